```python
import math
import jax
import jax.numpy as jnp
from jax import lax
import numpy as np

D_MODEL = 1024
BATCH = 4
SEQ = 4096
DEPTH = 4
DEC_BATCH = 128
DEC_SEQ = 1
PAST_LEN = 2048
PAGE_SIZE = 128

BR_W = D_MODEL // 2
N_BRANCH = 4
N_A = 64
H_A = BR_W // N_A
LORA_W = 64
LORA_A = 64
C_A_SHIFT = 3 * BR_W + LORA_W + LORA_A
RWKV_GN_EPS = 64e-5
H_B = 4
DK_B = 64
DV_B = 2 * DK_B
Q_BLOCK = 128
H_C = 4
DK_C = 128
DV_C = 128
CONV_W = 4
CHUNK = 64
N_MEM = 256
H_X = 4
HD_X = BR_W // H_X
NORM_EPS = 1e-6
NEG_INF = -1e30
SPLITS = (C_A_SHIFT, BR_W,
          2 * H_B * DK_B, 2 * H_B * DK_B, H_B * DV_B, BR_W,
          2 * H_C * DK_C + H_C * DV_C, H_C, H_C, BR_W,
          H_X * HD_X, BR_W,
          N_BRANCH * D_MODEL)
D_IN = sum(SPLITS)

kernel_name = 'hybrid_rwkv7_diffattn_gdn_decode_step'


def _split(x, sizes):
    parts, start = [], 0
    for s in sizes:
        parts.append(x[..., start:start + s])
        start += s
    return parts


def rms_norm(x, g, eps=NORM_EPS):
    x32 = x.astype(jnp.float32)
    y = x32 * lax.rsqrt(jnp.mean(x32 * x32, axis=-1, keepdims=True) + eps)
    return y.astype(x.dtype) * g


def l2_normalize(x, eps=1e-6):
    x32 = x.astype(jnp.float32)
    return (x32 * lax.rsqrt(jnp.sum(x32 * x32, axis=-1, keepdims=True) + eps)).astype(x.dtype)


def head_group_norm(y, g, b, eps):
    y32 = y.astype(jnp.float32)
    mu = jnp.mean(y32, axis=-1, keepdims=True)
    var = jnp.mean(jnp.square(y32 - mu), axis=-1, keepdims=True)
    yn = ((y32 - mu) * lax.rsqrt(var + eps)).astype(y.dtype)
    return yn.reshape(*y.shape[:-2], -1) * g + b


def rwkv7_scan(r, w, k, v, a, b, s0):
    def step(s, xs):
        r_t, w_t, k_t, v_t, a_t, b_t = xs
        sa = jnp.einsum('bhvk,bhk->bhv', s, a_t)
        s = s * w_t[:, :, None, :] + sa[..., None] * b_t[:, :, None, :] + v_t[..., None] * k_t[:, :, None, :]
        return s, jnp.einsum('bhvk,bhk->bhv', s, r_t)
    xs = tuple(jnp.moveaxis(z.astype(jnp.float32), 1, 0) for z in (r, w, k, v, a, b))
    s_fin, y = lax.scan(step, s0.astype(jnp.float32), xs)
    return jnp.moveaxis(y, 0, 1).astype(v.dtype), s_fin.astype(s0.dtype)


def rwkv_branch(p, shift_buf, s0, lp):
    bsz, t, _ = p.shape
    prev = jnp.concatenate([shift_buf, p[:, :-1]], axis=1)
    xm = p + (prev - p) * lp['rwkv_mu']
    r, k, v, lw, la = _split(xm, (BR_W, BR_W, BR_W, LORA_W, LORA_A))
    w_log = -jax.nn.softplus(-(lp['rwkv_w0'] + jnp.tanh(lw) @ lp['rwkv_w2'])) - 0.5
    decay = jnp.exp(-jnp.exp(w_log))
    a = jax.nn.sigmoid(lp['rwkv_a0'] + la @ lp['rwkv_a2'])
    hd = lambda z: z.reshape(bsz, t, H_A, N_A)
    kk = l2_normalize(hd(k * lp['rwkv_kk']))
    k = k * (1.0 + (a - 1.0) * lp['rwkv_ka'])
    r_h, k_h, v_h, a_h = hd(r), hd(k), hd(v), hd(a)
    y, s_new = rwkv7_scan(r_h, hd(decay), k_h, v_h, -kk, kk * a_h, s0)
    y = head_group_norm(y, lp['rwkv_ln_g'], lp['rwkv_ln_b'], RWKV_GN_EPS)
    bonus = jnp.sum(r_h * k_h * lp['rwkv_rk'], axis=-1, keepdims=True) * v_h
    y = y + bonus.reshape(bsz, t, BR_W)
    return y, s_new, p[:, -1:]


def diff_attend(q, k, v, q_pos, k_pos, lam):
    bsz, tk = k.shape[0], k.shape[1]
    k2 = k.reshape(bsz, tk, H_B, 2, DK_B)
    s = jnp.einsum('bqhcd,bkhcd->bhcqk', q, k2).astype(jnp.float32) * (DK_B ** -0.5)
    s = jnp.where(k_pos[None, :] <= q_pos[:, None], s, NEG_INF)
    p = jax.nn.softmax(s, axis=-1)
    attn = p[:, :, 0] - lam * p[:, :, 1]
    return jnp.einsum('bhqk,bkhd->bqhd', attn.astype(v.dtype), v)


def diff_branch(q, k, v, past_kv, lp, layer_idx):
    bsz, t, _ = q.shape
    q = q.reshape(bsz, t, H_B, 2, DK_B)
    k = k.reshape(bsz, t, H_B, 2 * DK_B)
    v = v.reshape(bsz, t, H_B, DV_B)
    lam_init = 0.8 - 0.6 * math.exp(-0.3 * layer_idx)
    lv = lp['diff_lam'].astype(jnp.float32)
    lam = jnp.exp(jnp.sum(lv[0] * lv[1])) - jnp.exp(jnp.sum(lv[2] * lv[3])) + lam_init
    if past_kv is None:
        n_blk = t // Q_BLOCK
        q_blocks = jnp.moveaxis(q.reshape(bsz, n_blk, Q_BLOCK, H_B, 2, DK_B), 1, 0)
        k_pos = jnp.arange(t)

        def one_block(args):
            i, q_i = args
            return diff_attend(q_i, k, v, i * Q_BLOCK + jnp.arange(Q_BLOCK), k_pos, lam)
        o = lax.map(one_block, (jnp.arange(n_blk), q_blocks))
        o = jnp.moveaxis(o, 0, 1).reshape(bsz, t, H_B, DV_B)
    else:
        past_k, past_v = past_kv
        p_len = past_k.shape[1]
        k_all = jnp.concatenate([past_k, k], axis=1)
        v_all = jnp.concatenate([past_v, v], axis=1)
        o = diff_attend(q, k_all, v_all, p_len + jnp.arange(t), jnp.arange(p_len + t), lam)
    o = rms_norm(o, lp['diff_norm_g'], eps=1e-5) * (1.0 - lam_init)
    return o.reshape(bsz, t, BR_W), k, v


def gated_delta_chunked(q, k, v, g, beta, s0):
    bsz, t, h, dk = q.shape
    dv = v.shape[-1]
    n = -(-t // CHUNK)
    pad = n * CHUNK - t

    def prep(z):
        z = z.astype(jnp.float32)
        z = jnp.pad(z, [(0, 0), (0, pad)] + [(0, 0)] * (z.ndim - 2))
        z = z.reshape(bsz, n, CHUNK, *z.shape[2:])
        return jnp.swapaxes(jnp.moveaxis(z, 1, 0), 2, 3)
    qc, kc, vc, gc, bc = (prep(z) for z in (q, k, v, g, beta))
    gcum = jnp.cumsum(gc, axis=-1)
    idx = jnp.arange(CHUNK)
    incl = idx[:, None] >= idx[None, :]
    strict = idx[:, None] > idx[None, :]
    diff = gcum[..., :, None] - gcum[..., None, :]
    decay = jnp.where(incl, jnp.exp(jnp.where(incl, diff, 0.0)), 0.0)
    kb = kc * bc[..., None]
    lower = jnp.where(strict, jnp.einsum('nbhid,nbhjd->nbhij', kb, kc) * decay, 0.0)
    a_mat = jnp.eye(CHUNK, dtype=jnp.float32) + lower
    rhs = jnp.concatenate([vc * bc[..., None], kb * jnp.exp(gcum)[..., None]], axis=-1)
    sol = lax.linalg.triangular_solve(a_mat, rhs, left_side=True, lower=True, unit_diagonal=True)
    u, w = sol[..., :dv], sol[..., dv:]
    qk = jnp.where(incl, jnp.einsum('nbhid,nbhjd->nbhij', qc, kc) * decay, 0.0)

    def step(s, xs):
        q_i, k_i, u_i, w_i, g_i, qk_i = xs
        v_new = u_i - jnp.einsum('bhcd,bhdv->bhcv', w_i, s)
        o_i = (jnp.einsum('bhcd,bhdv->bhcv', q_i * jnp.exp(g_i)[..., None], s)
               + jnp.einsum('bhij,bhjv->bhiv', qk_i, v_new))
        g_last = g_i[..., -1]
        s = (s * jnp.exp(g_last)[..., None, None]
             + jnp.einsum('bhcd,bhcv->bhdv', k_i * jnp.exp(g_last[..., None] - g_i)[..., None], v_new))
        return s, o_i
    s_fin, o = lax.scan(step, s0.astype(jnp.float32), (qc, kc, u, w, gcum, qk))
    o = jnp.moveaxis(jnp.swapaxes(o, 2, 3), 0, 1).reshape(bsz, n * CHUNK, h, dv)[:, :t]
    return o.astype(v.dtype), s_fin.astype(s0.dtype)


def gdn_branch(qkv, a, b, z, conv_buf, s0, lp):
    bsz, t, _ = qkv.shape
    full = jnp.concatenate([conv_buf, qkv], axis=1)
    wc = lp['gdn_conv_w']
    conv = full[:, 0:t] * wc[0]
    for i in range(1, CONV_W):
        conv = conv + full[:, i:i + t] * wc[i]
    conv = jax.nn.silu(conv)
    q, k, v = _split(conv, (H_C * DK_C, H_C * DK_C, H_C * DV_C))
    q = l2_normalize(q.reshape(bsz, t, H_C, DK_C)) * (DK_C ** -0.5)
    k = l2_normalize(k.reshape(bsz, t, H_C, DK_C))
    v = v.reshape(bsz, t, H_C, DV_C)
    beta = jax.nn.sigmoid(b)
    g = -jnp.exp(lp['gdn_a_log']) * jax.nn.softplus(a + lp['gdn_dt_bias'])
    o, s_new = gated_delta_chunked(q, k, v, g, beta, s0)
    o = rms_norm(o, lp['gdn_norm_g']) * jax.nn.silu(z.reshape(bsz, t, H_C, DV_C))
    return o.reshape(bsz, t, BR_W), s_new, full[:, t:]


def memory_kv(mem, lp):
    bsz, n_mem, _ = mem.shape
    m = rms_norm(mem, lp['norm_mem_g']) @ lp['w_mem_kv']
    mk, mv = _split(m, (H_X * HD_X, H_X * HD_X))
    return mk.reshape(bsz, n_mem, H_X, HD_X), mv.reshape(bsz, n_mem, H_X, HD_X)


def cross_branch(q, mem_k, mem_v):
    bsz, t, _ = q.shape
    q = q.reshape(bsz, t, H_X, HD_X)
    s = jnp.einsum('bqhd,bkhd->bhqk', q, mem_k).astype(jnp.float32) * (HD_X ** -0.5)
    p = jax.nn.softmax(s, axis=-1)
    o = jnp.einsum('bhqk,bkhd->bqhd', p.astype(mem_v.dtype), mem_v)
    return o.reshape(bsz, t, BR_W)


def trunk_layer(x, lp, layer_idx, mem_k, mem_v, past_kv, s_rwkv, s_shift, s_gdn, s_conv):
    bsz, t, _ = x.shape
    h = rms_norm(x, lp['norm_g'])
    proj = h @ lp['w_in']
    (a_sh, a_z, b_q, b_k, b_v, b_z, c_qkv, c_a, c_b, c_z, x_q, x_z, merge) = _split(proj, SPLITS)
    o_a, s_rwkv_new, shift_new = rwkv_branch(a_sh, s_shift, s_rwkv, lp)
    o_b, k_new, v_new = diff_branch(b_q, b_k, b_v, past_kv, lp, layer_idx)
    o_c, s_gdn_new, conv_new = gdn_branch(c_qkv, c_a, c_b, c_z, s_conv, s_gdn, lp)
    o_x = cross_branch(x_q, mem_k, mem_v)
    branches = jnp.stack([o_a * jax.nn.silu(a_z), o_b * jax.nn.silu(b_z), o_c,
                          o_x * jax.nn.silu(x_z)], axis=2)
    projected = jnp.einsum('btnc,ncd->btnd', branches, lp['w_branch'])
    gates = jax.nn.sigmoid(merge.reshape(bsz, t, N_BRANCH, D_MODEL))
    y = jnp.sum(gates * projected, axis=2) @ lp['w_out']
    return x + y, (k_new, v_new, s_rwkv_new, shift_new, s_gdn_new, conv_new)


def setup_inputs(seed: int = 0) -> dict:
    key = jax.random.key(seed)
    keys = iter(jax.random.split(key, 48))
    f32 = jnp.float32

    def normal(shape, scale):
        return jax.random.normal(next(keys), shape, f32) * scale

    def uniform(shape, lo, hi):
        return jax.random.uniform(next(keys), shape, f32, lo, hi)
    n_pages = PAST_LEN // PAGE_SIZE
    n_used = DEC_BATCH * n_pages
    n_phys = n_used + max(1, n_used // 4)
    page_table = jax.random.permutation(next(keys), n_phys)[:n_used].reshape(DEC_BATCH, n_pages).astype(jnp.int32)
    dt = jnp.exp(uniform((DEPTH, H_C), math.log(1e-3), math.log(1e-1)))
    return {
        'x_prompt': normal((BATCH, SEQ, D_MODEL), 1.0),
        'x_sample': normal((DEC_BATCH, DEC_SEQ, D_MODEL), 1.0),
        'mem_prompt': normal((BATCH, N_MEM, D_MODEL), 1.0),
        'cache_diff_k': normal((DEPTH, n_phys, PAGE_SIZE, H_B, 2 * DK_B), 1.0),
        'cache_diff_v': normal((DEPTH, n_phys, PAGE_SIZE, H_B, DV_B), 1.0),
        'page_table': page_table,
        'cache_mem_k': normal((DEPTH, DEC_BATCH, N_MEM, H_X, HD_X), 1.0),
        'cache_mem_v': normal((DEPTH, DEC_BATCH, N_MEM, H_X, HD_X), 1.0),
        'state_rwkv': normal((DEPTH, DEC_BATCH, H_A, N_A, N_A), 0.5),
        'state_rwkv_shift': normal((DEPTH, DEC_BATCH, 1, C_A_SHIFT), 1.0),
        'state_gdn': normal((DEPTH, DEC_BATCH, H_C, DK_C, DV_C), 0.1),
        'state_gdn_conv': normal((DEPTH, DEC_BATCH, CONV_W - 1, 3 * BR_W), 1.0),
        'norm_g': 1.0 + normal((DEPTH, D_MODEL), 0.02),
        'w_in': normal((DEPTH, D_MODEL, D_IN), D_MODEL ** -0.5),
        'rwkv_mu': uniform((DEPTH, C_A_SHIFT), 0.0, 1.0),
        'rwkv_w0': uniform((DEPTH, BR_W), -6.0, -1.0),
        'rwkv_w2': normal((DEPTH, LORA_W, BR_W), LORA_W ** -0.5),
        'rwkv_a0': normal((DEPTH, BR_W), 0.1),
        'rwkv_a2': normal((DEPTH, LORA_A, BR_W), LORA_A ** -0.5),
        'rwkv_kk': 0.85 + normal((DEPTH, BR_W), 0.02),
        'rwkv_ka': 1.0 + normal((DEPTH, BR_W), 0.02),
        'rwkv_rk': normal((DEPTH, H_A, N_A), 0.1),
        'rwkv_ln_g': 1.0 + normal((DEPTH, BR_W), 0.02),
        'rwkv_ln_b': normal((DEPTH, BR_W), 0.02),
        'diff_lam': normal((DEPTH, 4, DK_B), 0.1),
        'diff_norm_g': 1.0 + normal((DEPTH, DV_B), 0.02),
        'gdn_conv_w': normal((DEPTH, CONV_W, 3 * BR_W), CONV_W ** -0.5),
        'gdn_a_log': jnp.log(uniform((DEPTH, H_C), 1.0, 16.0)),
        'gdn_dt_bias': dt + jnp.log(-jnp.expm1(-dt)),
        'gdn_norm_g': 1.0 + normal((DEPTH, DV_C), 0.02),
        'norm_mem_g': 1.0 + normal((DEPTH, D_MODEL), 0.02),
        'w_mem_kv': normal((DEPTH, D_MODEL, 2 * H_X * HD_X), D_MODEL ** -0.5),
        'w_branch': normal((DEPTH, N_BRANCH, BR_W, D_MODEL), BR_W ** -0.5),
        'w_out': normal((DEPTH, D_MODEL, D_MODEL), D_MODEL ** -0.5),
        'final_norm_g': 1.0 + normal((D_MODEL,), 0.02),
    }


def reference(x_prompt, x_sample, mem_prompt, cache_diff_k, cache_diff_v, page_table,
              cache_mem_k, cache_mem_v, state_rwkv, state_rwkv_shift, state_gdn, state_gdn_conv,
              norm_g, w_in, rwkv_mu, rwkv_w0, rwkv_w2, rwkv_a0, rwkv_a2, rwkv_kk, rwkv_ka, rwkv_rk,
              rwkv_ln_g, rwkv_ln_b, diff_lam, diff_norm_g, gdn_conv_w, gdn_a_log, gdn_dt_bias,
              gdn_norm_g, norm_mem_g, w_mem_kv, w_branch, w_out, final_norm_g):
    params = {'norm_g': norm_g, 'w_in': w_in, 'rwkv_mu': rwkv_mu, 'rwkv_w0': rwkv_w0, 'rwkv_w2': rwkv_w2,
              'rwkv_a0': rwkv_a0, 'rwkv_a2': rwkv_a2, 'rwkv_kk': rwkv_kk, 'rwkv_ka': rwkv_ka, 'rwkv_rk': rwkv_rk,
              'rwkv_ln_g': rwkv_ln_g, 'rwkv_ln_b': rwkv_ln_b, 'diff_lam': diff_lam, 'diff_norm_g': diff_norm_g,
              'gdn_conv_w': gdn_conv_w, 'gdn_a_log': gdn_a_log, 'gdn_dt_bias': gdn_dt_bias,
              'gdn_norm_g': gdn_norm_g, 'norm_mem_g': norm_mem_g, 'w_mem_kv': w_mem_kv,
              'w_branch': w_branch, 'w_out': w_out}
    bsz = x_prompt.shape[0]
    dbsz = x_sample.shape[0]
    n_pages = page_table.shape[1]
    dt = x_prompt.dtype
    zero_rwkv = jnp.zeros((bsz, H_A, N_A, N_A), dt)
    zero_shift = jnp.zeros((bsz, 1, C_A_SHIFT), dt)
    zero_gdn = jnp.zeros((bsz, H_C, DK_C, DV_C), dt)
    zero_conv = jnp.zeros((bsz, CONV_W - 1, 3 * BR_W), dt)
    xp, xs = x_prompt, x_sample
    kp, vp, ks, vs, mkp, mvp = [], [], [], [], [], []
    rp, rs, shp, shs, gp, gs, cp, cs = [], [], [], [], [], [], [], []
    for l in range(DEPTH):
        lp = {name: arr[l] for name, arr in params.items()}
        mk_p, mv_p = memory_kv(mem_prompt, lp)
        xp, st_p = trunk_layer(xp, lp, l, mk_p, mv_p, None, zero_rwkv, zero_shift, zero_gdn, zero_conv)
        past_k = cache_diff_k[l, page_table].reshape(dbsz, n_pages * PAGE_SIZE, H_B, 2 * DK_B)
        past_v = cache_diff_v[l, page_table].reshape(dbsz, n_pages * PAGE_SIZE, H_B, DV_B)
        xs, st_s = trunk_layer(xs, lp, l, cache_mem_k[l], cache_mem_v[l], (past_k, past_v),
                               state_rwkv[l], state_rwkv_shift[l], state_gdn[l], state_gdn_conv[l])
        kp.append(st_p[0]); vp.append(st_p[1]); ks.append(st_s[0]); vs.append(st_s[1])
        mkp.append(mk_p); mvp.append(mv_p)
        rp.append(st_p[2]); rs.append(st_s[2]); shp.append(st_p[3]); shs.append(st_s[3])
        gp.append(st_p[4]); gs.append(st_s[4]); cp.append(st_p[5]); cs.append(st_s[5])
    y_prompt = rms_norm(xp, final_norm_g)
    y_sample = rms_norm(xs, final_norm_g)
    diff_k_prompt = jnp.stack(kp)
    diff_v_prompt = jnp.stack(vp)
    diff_k_sample = jnp.stack(ks)
    diff_v_sample = jnp.stack(vs)
    mem_k_prompt = jnp.stack(mkp)
    mem_v_prompt = jnp.stack(mvp)
    rwkv_prompt = jnp.stack(rp)
    rwkv_sample = jnp.stack(rs)
    rwkv_shift_prompt = jnp.stack(shp)
    rwkv_shift_sample = jnp.stack(shs)
    gdn_prompt = jnp.stack(gp)
    gdn_sample = jnp.stack(gs)
    gdn_conv_prompt = jnp.stack(cp)
    gdn_conv_sample = jnp.stack(cs)
    return (y_prompt, y_sample, diff_k_prompt, diff_v_prompt, diff_k_sample, diff_v_sample,
            mem_k_prompt, mem_v_prompt, rwkv_prompt, rwkv_sample, rwkv_shift_prompt, rwkv_shift_sample,
            gdn_prompt, gdn_sample, gdn_conv_prompt, gdn_conv_sample)
```

```python
import functools
import math

import jax
import jax.numpy as jnp
from jax import lax
from jax.experimental import pallas as pl
from jax.experimental.pallas import tpu as pltpu

F32 = jnp.float32
BF16 = jnp.bfloat16
HI = lax.Precision.HIGHEST

D_MODEL = 1024
BR_W = D_MODEL // 2
N_BRANCH = 4
N_A = 64
H_A = BR_W // N_A
LORA = 64
H_B = 4
DK_B = 64
DV_B = 128
H_C = 4
DK_C = 128
DV_C = 128
CONV_W = 4
CHUNK = 64
H_X = 4
HD_X = BR_W // H_X
PAGE = 128
RWKV_GN_EPS = 64e-5
NORM_EPS = 1e-6
NEG_INF = -1e30
LANES = 128
SUBLANES = 8
VMEM_LIMIT = 48 * 1024 * 1024

W512 = ('a_r', 'a_k', 'a_v', 'a_z', 'b_q', 'b_k', 'b_v', 'b_z', 'c_q', 'c_k', 'c_v', 'c_z', 'x_q', 'x_z')
MERGE_W = N_BRANCH * D_MODEL
BLK512 = {name: MERGE_W // BR_W + i for i, name in enumerate(W512)}
LORA_BLK = (MERGE_W + len(W512) * BR_W) // LANES
CAB_BLK = LORA_BLK + 1
PROJ_W = (CAB_BLK + 1) * LANES
REF_OFF = {'a_r': 0, 'a_k': 512, 'a_v': 1024, 'a_lora': 1536, 'a_z': 1664, 'b_q': 2176, 'b_k': 2688, 'b_v': 3200,
           'b_z': 3712, 'c_q': 4224, 'c_k': 4736, 'c_v': 5248, 'c_ab': 5760, 'c_z': 5768, 'x_q': 6280, 'x_z': 6792,
           'merge': 7304}


def _cparams(*sem):
    return pltpu.CompilerParams(dimension_semantics=sem, vmem_limit_bytes=VMEM_LIMIT)


def _sigmoid(x):
    return 1.0 / (1.0 + jnp.exp(-x))


def _silu(x):
    return x * _sigmoid(x)


def _softplus(x):
    return jnp.maximum(x, 0.0) + jnp.log(1.0 + jnp.exp(-jnp.abs(x)))


def _dot(a, b, precision=None):
    return jnp.dot(a, b, preferred_element_type=F32, precision=precision)


def _dot_nt(a, b, precision=None):
    return lax.dot_general(a, b, (((1,), (1,)), ((), ())), preferred_element_type=F32, precision=precision)


def _iota(shape, axis):
    return lax.broadcasted_iota(jnp.int32, shape, axis)


def _seg_allsum64(x):
    ax = x.ndim - 1
    n = x.shape[ax]
    lane = _iota(x.shape, ax)
    for s in (32, 16, 8, 4, 2, 1):
        up = pltpu.roll(x, s, ax)
        dn = pltpu.roll(x, n - s, ax)
        x = x + jnp.where((lane & s) != 0, up, dn)
    return x


def _tri_inverse(neg_l):
    n = neg_l.shape[0]
    eye = (_iota((n, n), 0) == _iota((n, n), 1)).astype(F32)
    x = eye + neg_l
    p = neg_l
    for _ in range(5):
        p = _dot(p, p, HI)
        x = x + _dot(x, p, HI)
    return x


def _norm_matmul_kernel(x_ref, g_ref, w_ref, o_ref, h_ref):
    @pl.when(pl.program_id(1) == 0)
    def _():
        x = x_ref[...]
        y = x * lax.rsqrt(jnp.mean(x * x, axis=-1, keepdims=True) + NORM_EPS)
        h_ref[...] = (y * g_ref[...]).astype(BF16)
    o_ref[...] = _dot(h_ref[...], w_ref[...])


def _norm_matmul(x, g, w, tm, tn):
    n, d = x.shape
    c = w.shape[1]
    return pl.pallas_call(
        _norm_matmul_kernel,
        grid=(n // tm, c // tn),
        in_specs=[pl.BlockSpec((tm, d), lambda i, j: (i, 0)),
                  pl.BlockSpec((1, d), lambda i, j: (0, 0)),
                  pl.BlockSpec((d, tn), lambda i, j: (0, j))],
        out_specs=pl.BlockSpec((tm, tn), lambda i, j: (i, j)),
        out_shape=jax.ShapeDtypeStruct((n, c), F32),
        scratch_shapes=[pltpu.VMEM((tm, d), BF16)],
        compiler_params=_cparams("parallel", "arbitrary"),
    )(x, g.reshape(1, d), w)


def _final_norm_kernel(x_ref, g_ref, o_ref):
    x = x_ref[...]
    o_ref[...] = x * lax.rsqrt(jnp.mean(x * x, axis=-1, keepdims=True) + NORM_EPS) * g_ref[...]


def _final_norm(x, g, tm):
    n, d = x.shape
    return pl.pallas_call(
        _final_norm_kernel,
        grid=(n // tm,),
        in_specs=[pl.BlockSpec((tm, d), lambda i: (i, 0)), pl.BlockSpec((1, d), lambda i: (0, 0))],
        out_specs=pl.BlockSpec((tm, d), lambda i: (i, 0)),
        out_shape=jax.ShapeDtypeStruct((n, d), F32),
        compiler_params=_cparams("parallel"),
    )(x, g.reshape(1, d))


def _rwkv_prep_math(cur, prev, prm):
    mu_r, mu_k, mu_v, mu_l, w0, w2p, a0, a2p, kkp, kap = prm
    r = cur[0] + (prev[0] - cur[0]) * mu_r
    k = cur[1] + (prev[1] - cur[1]) * mu_k
    v = cur[2] + (prev[2] - cur[2]) * mu_v
    lo = cur[3] + (prev[3] - cur[3]) * mu_l
    w_log = -_softplus(-(w0 + _dot(jnp.tanh(lo), w2p))) - 0.5
    log_decay = -jnp.exp(w_log)
    a = _sigmoid(a0 + _dot(lo, a2p))
    kk = k * kkp
    kk = kk * lax.rsqrt(_seg_allsum64(kk * kk) + 1e-6)
    k = k * (1.0 + (a - 1.0) * kap)
    return r, log_decay, k, v, -kk, kk * a


def _rwkv_prep_prompt_kernel(pr, pk, pv, plo, mu_r, mu_k, mu_v, mu_l, w0, w2p, a0, a2p, kkp, kap,
                             o_r, o_w, o_k, o_v, o_a, o_b, c_r, c_k, c_v, c_l):
    carries = (c_r, c_k, c_v, c_l)

    @pl.when(pl.program_id(1) == 0)
    def _():
        for c in carries:
            c[...] = jnp.zeros_like(c)
    cur = (pr[...], pk[...], pv[...], plo[...])
    prev = []
    for p, c in zip(cur, carries):
        row = _iota(p.shape, 0)
        prev.append(jnp.where(row == 0, c[...], pltpu.roll(p, 1, 0)))
    prm = tuple(z[...] for z in (mu_r, mu_k, mu_v, mu_l, w0, w2p, a0, a2p, kkp, kap))
    outs = _rwkv_prep_math(cur, prev, prm)
    for o, val in zip((o_r, o_w, o_k, o_v, o_a, o_b), outs):
        o[...] = val
    tm = cur[0].shape[0]
    for p, c in zip(cur, carries):
        c[...] = p[tm - 1:tm, :]


def _rwkv_prep_sample_kernel(pr, pk, pv, plo, qr, qk, qv, qlo, mu_r, mu_k, mu_v, mu_l, w0, w2p, a0, a2p, kkp, kap,
                             o_r, o_w, o_k, o_v, o_a, o_b):
    cur = (pr[...], pk[...], pv[...], plo[...])
    prev = (qr[...], qk[...], qv[...], qlo[...])
    prm = tuple(z[...] for z in (mu_r, mu_k, mu_v, mu_l, w0, w2p, a0, a2p, kkp, kap))
    outs = _rwkv_prep_math(cur, prev, prm)
    for o, val in zip((o_r, o_w, o_k, o_v, o_a, o_b), outs):
        o[...] = val


def _full_spec(a):
    nd = a.ndim
    return pl.BlockSpec(a.shape, lambda *_: (0,) * nd)


def _rwkv_prep_prompt(proj, prm, tm):
    bsz, t, _ = proj.shape

    def col(blk, w):
        return pl.BlockSpec((None, tm, w), lambda b, i: (b, i, blk))
    out = pl.BlockSpec((None, tm, BR_W), lambda b, i: (b, i, 0))
    return pl.pallas_call(
        _rwkv_prep_prompt_kernel,
        grid=(bsz, t // tm),
        in_specs=[col(BLK512['a_r'], BR_W), col(BLK512['a_k'], BR_W), col(BLK512['a_v'], BR_W), col(LORA_BLK, LANES)]
                 + [_full_spec(p) for p in prm],
        out_specs=[out] * 6,
        out_shape=[jax.ShapeDtypeStruct((bsz, t, BR_W), F32)] * 6,
        scratch_shapes=[pltpu.VMEM((1, BR_W), F32)] * 3 + [pltpu.VMEM((1, LANES), F32)],
        compiler_params=_cparams("parallel", "arbitrary"),
    )(proj, proj, proj, proj, *prm)


def _rwkv_prep_sample(proj, shift, prm):
    n = proj.shape[0]

    def col(blk, w):
        return pl.BlockSpec((n, w), lambda i: (0, blk))
    out = pl.BlockSpec((n, BR_W), lambda i: (0, 0))
    return pl.pallas_call(
        _rwkv_prep_sample_kernel,
        grid=(1,),
        in_specs=[col(BLK512['a_r'], BR_W), col(BLK512['a_k'], BR_W), col(BLK512['a_v'], BR_W), col(LORA_BLK, LANES)]
                 + [_full_spec(s) for s in shift] + [_full_spec(p) for p in prm],
        out_specs=[out] * 6,
        out_shape=[jax.ShapeDtypeStruct((n, BR_W), F32)] * 6,
        compiler_params=_cparams("arbitrary"),
    )(proj, proj, proj, proj, *shift, *prm)


def _stack_pair(x):
    lane = _iota(x.shape, 1)
    return jnp.concatenate([jnp.where(lane < N_A, x, 0.0), jnp.where(lane >= N_A, x, 0.0)], axis=0)


def _rwkv_group_norm_bonus(y, r, k, v, g, b, rk):
    mu = _seg_allsum64(y) * (1.0 / N_A)
    d = y - mu
    var = _seg_allsum64(d * d) * (1.0 / N_A)
    yn = d * lax.rsqrt(var + RWKV_GN_EPS) * g + b
    return yn + _seg_allsum64(r * k * rk) * v


def _rwkv_chunk_kernel(r_ref, w_ref, k_ref, v_ref, a_ref, b_ref, g_ref, bb_ref, rk_ref, o_ref, s_ref, st_ref):
    ci = pl.program_id(1)
    nc = pl.num_programs(1)
    c = CHUNK

    @pl.when(ci == 0)
    def _():
        st_ref[...] = jnp.zeros_like(st_ref)
    lw = w_ref[...]
    tri = (_iota((c, c), 0) >= _iota((c, c), 1)).astype(F32)
    cum = _dot(tri, lw, HI)
    p_incl = jnp.exp(cum)
    p_inv = jnp.exp(-cum)
    p_prev = jnp.exp(cum - lw)
    p_last = p_incl[c - 1:c, :]
    r = r_ref[...]
    k = k_ref[...]
    v = v_ref[...]
    at = a_ref[...] * p_prev
    bt = b_ref[...] * p_inv
    kt = k * p_inv
    rt = r * p_incl
    bh = bt * p_last
    kh = kt * p_last
    n2 = 2 * c
    ri = _iota((n2, n2), 0)
    cj = _iota((n2, n2), 1)
    same = (ri >= c) == (cj >= c)
    strict = same & ((ri & (c - 1)) > (cj & (c - 1)))
    incl = same & ((ri & (c - 1)) >= (cj & (c - 1)))
    eye = (ri == cj).astype(F32)
    for p in range(H_A // 2):
        sl = slice(p * LANES, (p + 1) * LANES)
        a_s, b_s, k_s, r_s, v_s = (_stack_pair(z[:, sl]) for z in (at, bt, kt, rt, v))
        bh_s, kh_s = _stack_pair(bh[:, sl]), _stack_pair(kh[:, sl])
        l_ab = jnp.where(strict, _dot_nt(a_s, b_s, HI), 0.0)
        l_ak = jnp.where(strict, _dot_nt(a_s, k_s, HI), 0.0)
        m_rb = jnp.where(incl, _dot_nt(r_s, b_s, HI), 0.0)
        m_rk = jnp.where(incl, _dot_nt(r_s, k_s, HI), 0.0)
        t_inv = _tri_inverse(l_ab)
        w_t = _dot(t_inv, a_s, HI)
        u_t = _dot(t_inv, _dot(l_ak, v_s, HI), HI)
        s_t = st_ref[p]
        u = _dot(w_t, s_t, HI) + u_t
        y_s = _dot(r_s, s_t, HI) + _dot(m_rb, u, HI) + _dot(m_rk, v_s, HI)
        y = y_s[:c] + y_s[c:]
        p_col = jnp.sum(eye * p_last[:, sl], axis=1, keepdims=True)
        s_new = s_t * p_col + _dot(bh_s.T, u, HI) + _dot(kh_s.T, v_s, HI)
        st_ref[p] = s_new
        o_ref[:, sl] = _rwkv_group_norm_bonus(y, r[:, sl], k[:, sl], v[:, sl], g_ref[:, sl], bb_ref[:, sl],
                                              rk_ref[:, sl])

        @pl.when(ci == nc - 1)
        def _():
            s_vk = s_new.T
            s_ref[2 * p] = s_vk[:N_A, :N_A]
            s_ref[2 * p + 1] = s_vk[N_A:, N_A:]


def _rwkv_chunk(r, w, k, v, a, b, ln_g, ln_b, rk):
    bsz, t, _ = r.shape
    row = pl.BlockSpec((None, CHUNK, BR_W), lambda bi, ci: (bi, ci, 0))
    par = pl.BlockSpec((1, BR_W), lambda bi, ci: (0, 0))
    return pl.pallas_call(
        _rwkv_chunk_kernel,
        grid=(bsz, t // CHUNK),
        in_specs=[row] * 6 + [par] * 3,
        out_specs=[row, pl.BlockSpec((None, H_A, N_A, N_A), lambda bi, ci: (bi, 0, 0, 0))],
        out_shape=[jax.ShapeDtypeStruct((bsz, t, BR_W), F32), jax.ShapeDtypeStruct((bsz, H_A, N_A, N_A), F32)],
        scratch_shapes=[pltpu.VMEM((H_A // 2, LANES, LANES), F32)],
        compiler_params=_cparams("parallel", "arbitrary"),
    )(r, w, k, v, a, b, ln_g, ln_b, rk)


def _rwkv_step_kernel(r_ref, w_ref, k_ref, v_ref, a_ref, b_ref, g_ref, bb_ref, rk_ref, s_ref, o_ref, so_ref):
    nb = s_ref.shape[0]
    eye = (_iota((N_A, N_A), 0) == _iota((N_A, N_A), 1)).astype(F32)

    def body(bi, carry):
        for h in range(H_A):
            s = s_ref[bi, h]
            r, k, v = r_ref[bi, h], k_ref[bi, h], v_ref[bi, h]
            sa = jnp.sum(s * a_ref[bi, h], axis=1, keepdims=True)
            v_col = jnp.sum(eye * v, axis=1, keepdims=True)
            s2 = s * jnp.exp(w_ref[bi, h]) + sa * b_ref[bi, h] + v_col * k
            y_col = jnp.sum(s2 * r, axis=1, keepdims=True)
            y = jnp.sum(eye * y_col, axis=0, keepdims=True)
            mu = jnp.mean(y, axis=1, keepdims=True)
            d = y - mu
            var = jnp.mean(d * d, axis=1, keepdims=True)
            yn = d * lax.rsqrt(var + RWKV_GN_EPS) * g_ref[h] + bb_ref[h]
            o_ref[bi, h] = yn + jnp.sum(r * k * rk_ref[h], axis=1, keepdims=True) * v
            so_ref[bi, h] = s2
        return carry
    lax.fori_loop(0, nb, body, 0)


def _rwkv_step(rows, ln_g, ln_b, rk, state, layer, nb):
    n = rows[0].shape[0]
    rows = [z.reshape(n, H_A, 1, N_A) for z in rows]
    prm = [z.reshape(H_A, 1, N_A) for z in (ln_g, ln_b, rk)]
    row = pl.BlockSpec((nb, H_A, 1, N_A), lambda i: (i, 0, 0, 0))
    par = pl.BlockSpec((H_A, 1, N_A), lambda i: (0, 0, 0))
    st_in = pl.BlockSpec((None, nb, H_A, N_A, N_A), lambda i: (layer, i, 0, 0, 0))
    st_out = pl.BlockSpec((nb, H_A, N_A, N_A), lambda i: (i, 0, 0, 0))
    o, s_new = pl.pallas_call(
        _rwkv_step_kernel,
        grid=(n // nb,),
        in_specs=[row] * 6 + [par] * 3 + [st_in],
        out_specs=[row, st_out],
        out_shape=[jax.ShapeDtypeStruct((n, H_A, 1, N_A), F32), jax.ShapeDtypeStruct((n, H_A, N_A, N_A), F32)],
        compiler_params=_cparams("parallel"),
    )(*rows, *prm, state)
    return o.reshape(n, BR_W), s_new


def _gdn_norm_qk(q, k):
    qs, ks = [], []
    for h in range(H_C):
        sl = slice(h * DK_C, (h + 1) * DK_C)
        qh, kh = q[:, sl], k[:, sl]
        qs.append(qh * lax.rsqrt(jnp.sum(qh * qh, axis=1, keepdims=True) + 1e-6) * (DK_C ** -0.5))
        ks.append(kh * lax.rsqrt(jnp.sum(kh * kh, axis=1, keepdims=True) + 1e-6))
    return jnp.concatenate(qs, axis=1), jnp.concatenate(ks, axis=1)


def _gdn_gate(ab, alog, dtb):
    lane = _iota(ab.shape, 1)
    return jnp.where(lane < H_C, -jnp.exp(alog) * _softplus(ab + dtb), _sigmoid(ab))


def _gdn_prep_prompt_kernel(xq, xk, xv, xab, wq, wk, wv, alog, dtb, o_q, o_k, o_v, o_gb, t_q, t_k, t_v):
    tails = (t_q, t_k, t_v)

    @pl.when(pl.program_id(1) == 0)
    def _():
        for tl in tails:
            tl[...] = jnp.zeros_like(tl)
    outs = []
    for x_ref, w_ref, tl in zip((xq, xk, xv), (wq, wk, wv), tails):
        x = x_ref[...]
        tm = x.shape[0]
        tail = tl[...]
        acc = x * w_ref[CONV_W - 1:CONV_W, :]
        row8 = _iota(tail.shape, 0)
        for s in range(1, CONV_W):
            rolled = pltpu.roll(x, s, 0)
            head = jnp.where(row8 < s, pltpu.roll(tail, s, 0), rolled[:SUBLANES])
            xs = jnp.concatenate([head, rolled[SUBLANES:]], axis=0)
            acc = acc + xs * w_ref[CONV_W - 1 - s:CONV_W - s, :]
        outs.append(_silu(acc))
        tl[...] = x[tm - SUBLANES:, :]
    q, k = _gdn_norm_qk(outs[0], outs[1])
    o_q[...] = q
    o_k[...] = k
    o_v[...] = outs[2]
    o_gb[...] = _gdn_gate(xab[...], alog[...], dtb[...])


def _gdn_prep_prompt(proj, prm, tm):
    bsz, t, _ = proj.shape

    def col(blk, w):
        return pl.BlockSpec((None, tm, w), lambda b, i: (b, i, blk))
    out = pl.BlockSpec((None, tm, BR_W), lambda b, i: (b, i, 0))
    outg = pl.BlockSpec((None, tm, LANES), lambda b, i: (b, i, 0))
    return pl.pallas_call(
        _gdn_prep_prompt_kernel,
        grid=(bsz, t // tm),
        in_specs=[col(BLK512['c_q'], BR_W), col(BLK512['c_k'], BR_W), col(BLK512['c_v'], BR_W), col(CAB_BLK, LANES)]
                 + [_full_spec(p) for p in prm],
        out_specs=[out, out, out, outg],
        out_shape=[jax.ShapeDtypeStruct((bsz, t, BR_W), F32)] * 3 + [jax.ShapeDtypeStruct((bsz, t, LANES), F32)],
        scratch_shapes=[pltpu.VMEM((SUBLANES, BR_W), F32)] * 3,
        compiler_params=_cparams("parallel", "arbitrary"),
    )(proj, proj, proj, proj, *prm)


def _gdn_prep_sample_kernel(xq, xk, xv, xab, bq, bk, bv, wq, wk, wv, alog, dtb, o_q, o_k, o_v, o_eg, o_beta):
    outs = []
    for x_ref, b_ref, w_ref in zip((xq, xk, xv), (bq, bk, bv), (wq, wk, wv)):
        acc = x_ref[...] * w_ref[CONV_W - 1:CONV_W, :]
        for i in range(CONV_W - 1):
            acc = acc + b_ref[i] * w_ref[i:i + 1, :]
        outs.append(_silu(acc))
    q, k = _gdn_norm_qk(outs[0], outs[1])
    o_q[...] = q
    o_k[...] = k
    o_v[...] = outs[2]
    gb = _gdn_gate(xab[...], alog[...], dtb[...])
    n = gb.shape[0]
    eg = jnp.exp(gb)
    o_eg[...] = jnp.concatenate([jnp.broadcast_to(eg[:, h:h + 1], (n, DV_C)) for h in range(H_C)], axis=1)
    o_beta[...] = jnp.concatenate([jnp.broadcast_to(gb[:, H_C + h:H_C + h + 1], (n, DV_C)) for h in range(H_C)],
                                  axis=1)


def _gdn_prep_sample(proj, bufs, prm):
    n = proj.shape[0]

    def col(blk, w):
        return pl.BlockSpec((n, w), lambda i: (0, blk))
    out = pl.BlockSpec((n, BR_W), lambda i: (0, 0))
    return pl.pallas_call(
        _gdn_prep_sample_kernel,
        grid=(1,),
        in_specs=[col(BLK512['c_q'], BR_W), col(BLK512['c_k'], BR_W), col(BLK512['c_v'], BR_W), col(CAB_BLK, LANES)]
                 + [_full_spec(z) for z in bufs] + [_full_spec(p) for p in prm],
        out_specs=[out] * 5,
        out_shape=[jax.ShapeDtypeStruct((n, BR_W), F32)] * 5,
        compiler_params=_cparams("arbitrary"),
    )(proj, proj, proj, proj, *bufs, *prm)


def _gdn_out_norm(o, g, z):
    return o * lax.rsqrt(jnp.mean(o * o, axis=1, keepdims=True) + NORM_EPS) * g * _silu(z)


def _gdn_chunk_kernel(q_ref, k_ref, v_ref, gb_ref, z_ref, g_ref, o_ref, s_ref, st_ref):
    ci = pl.program_id(1)
    nc = pl.num_programs(1)
    c = CHUNK

    @pl.when(ci == 0)
    def _():
        st_ref[...] = jnp.zeros_like(st_ref)
    gb = gb_ref[...]
    ri = _iota((c, c), 0)
    cj = _iota((c, c), 1)
    incl = ri >= cj
    strict = ri > cj
    cum = _dot(incl.astype(F32), gb, HI)
    cum_t = cum.T
    for h in range(H_C):
        sl = slice(h * DK_C, (h + 1) * DK_C)
        gc_col = cum[:, h:h + 1]
        gc_row = cum_t[h:h + 1, :]
        g_last = cum[c - 1:c, h:h + 1]
        beta = gb[:, H_C + h:H_C + h + 1]
        decay = jnp.where(incl, jnp.exp(jnp.where(incl, gc_col - gc_row, 0.0)), 0.0)
        q, k, v = q_ref[:, sl], k_ref[:, sl], v_ref[:, sl]
        kb = k * beta
        lower = jnp.where(strict, _dot_nt(kb, k) * decay, 0.0)
        t_inv = _tri_inverse(-lower)
        u = _dot(t_inv, v * beta)
        w = _dot(t_inv, kb * jnp.exp(gc_col))
        qk = jnp.where(incl, _dot_nt(q, k) * decay, 0.0)
        s = st_ref[h]
        v_new = u - _dot(w, s)
        o = _dot(q * jnp.exp(gc_col), s) + _dot(qk, v_new)
        s_new = s * jnp.exp(g_last) + _dot((k * jnp.exp(g_last - gc_col)).T, v_new)
        st_ref[h] = s_new
        o_ref[:, sl] = _gdn_out_norm(o, g_ref[...], z_ref[:, sl])

        @pl.when(ci == nc - 1)
        def _():
            s_ref[h] = s_new


def _gdn_chunk(q, k, v, gb, proj, norm_g):
    bsz, t, _ = q.shape
    row = pl.BlockSpec((None, CHUNK, BR_W), lambda bi, ci: (bi, ci, 0))
    return pl.pallas_call(
        _gdn_chunk_kernel,
        grid=(bsz, t // CHUNK),
        in_specs=[row, row, row,
                  pl.BlockSpec((None, CHUNK, LANES), lambda bi, ci: (bi, ci, 0)),
                  pl.BlockSpec((None, CHUNK, BR_W), lambda bi, ci: (bi, ci, BLK512['c_z'])),
                  pl.BlockSpec((1, DV_C), lambda bi, ci: (0, 0))],
        out_specs=[row, pl.BlockSpec((None, H_C, DK_C, DV_C), lambda bi, ci: (bi, 0, 0, 0))],
        out_shape=[jax.ShapeDtypeStruct((bsz, t, BR_W), F32), jax.ShapeDtypeStruct((bsz, H_C, DK_C, DV_C), F32)],
        scratch_shapes=[pltpu.VMEM((H_C, DK_C, DV_C), F32)],
        compiler_params=_cparams("parallel", "arbitrary"),
    )(q, k, v, gb, proj, norm_g.reshape(1, DV_C))


def _gdn_step_kernel(q_ref, k_ref, v_ref, eg_ref, beta_ref, z_ref, g_ref, s_ref, o_ref, so_ref):
    nb = s_ref.shape[0]
    eye = (_iota((DK_C, DK_C), 0) == _iota((DK_C, DK_C), 1)).astype(F32)

    def body(bi, carry):
        for h in range(H_C):
            s = s_ref[bi, h]
            eg = eg_ref[bi, h]
            k_col = jnp.sum(eye * k_ref[bi, h], axis=1, keepdims=True)
            q_col = jnp.sum(eye * q_ref[bi, h], axis=1, keepdims=True)
            ks = jnp.sum(s * k_col, axis=0, keepdims=True)
            u = beta_ref[bi, h] * (v_ref[bi, h] - eg * ks)
            s2 = s * eg + k_col * u
            o = jnp.sum(s2 * q_col, axis=0, keepdims=True)
            o_ref[bi, h] = _gdn_out_norm(o, g_ref[...], z_ref[bi, h])
            so_ref[bi, h] = s2
        return carry
    lax.fori_loop(0, nb, body, 0)


def _gdn_step(q, k, v, eg, beta, z, norm_g, state, layer, nb):
    n = q.shape[0]
    rows = [x.reshape(n, H_C, 1, DV_C) for x in (q, k, v, eg, beta, z)]
    row = pl.BlockSpec((nb, H_C, 1, DV_C), lambda i: (i, 0, 0, 0))
    st_in = pl.BlockSpec((None, nb, H_C, DK_C, DV_C), lambda i: (layer, i, 0, 0, 0))
    st_out = pl.BlockSpec((nb, H_C, DK_C, DV_C), lambda i: (i, 0, 0, 0))
    o, s_new = pl.pallas_call(
        _gdn_step_kernel,
        grid=(n // nb,),
        in_specs=[row] * 6 + [pl.BlockSpec((1, DV_C), lambda i: (0, 0)), st_in],
        out_specs=[row, st_out],
        out_shape=[jax.ShapeDtypeStruct((n, H_C, 1, DV_C), F32), jax.ShapeDtypeStruct((n, H_C, DK_C, DV_C), F32)],
        compiler_params=_cparams("parallel"),
    )(*rows, norm_g.reshape(1, DV_C), state)
    return o.reshape(n, BR_W), s_new


def _diff_lambda(lam_ref, lam_init):
    lv = lam_ref[...]
    s1 = jnp.sum(lv[0:1] * lv[1:2], axis=1, keepdims=True)
    s2 = jnp.sum(lv[2:3] * lv[3:4], axis=1, keepdims=True)
    return jnp.exp(s1) - jnp.exp(s2) + lam_init


def _diff_out_norm(o, g, lam_init):
    return o * lax.rsqrt(jnp.mean(o * o, axis=1, keepdims=True) + 1e-5) * g * (1.0 - lam_init)


def _diff_flash_kernel(q_ref, k_ref, v_ref, lam_ref, g_ref, o_ref, qs_ref, m_ref, l_ref, acc_ref, *, lam_init):
    i = pl.program_id(2)
    j = pl.program_id(3)
    tq = q_ref.shape[0]
    tk = k_ref.shape[0]

    @pl.when(j == 0)
    def _():
        q = q_ref[...] * (DK_B ** -0.5)
        lane = _iota(q.shape, 1)
        qs_ref[:tq] = jnp.where(lane < DK_B, q, 0.0).astype(BF16)
        qs_ref[tq:] = jnp.where(lane >= DK_B, q, 0.0).astype(BF16)
        m_ref[...] = jnp.full_like(m_ref, NEG_INF)
        l_ref[...] = jnp.zeros_like(l_ref)
        acc_ref[...] = jnp.zeros_like(acc_ref)

    @pl.when(j <= i)
    def _():
        s = _dot_nt(qs_ref[...], k_ref[...].astype(BF16))
        q_pos = i * tq + (_iota(s.shape, 0) & (tq - 1))
        k_pos = j * tk + _iota(s.shape, 1)
        s = jnp.where(k_pos <= q_pos, s, NEG_INF)
        m_old = m_ref[...]
        m_new = jnp.maximum(m_old, jnp.max(s, axis=1, keepdims=True))
        alpha = jnp.exp(m_old - m_new)
        p = jnp.exp(s - m_new)
        l_ref[...] = alpha * l_ref[...] + jnp.sum(p, axis=1, keepdims=True)
        acc_ref[...] = alpha * acc_ref[...] + _dot(p.astype(BF16), v_ref[...].astype(BF16))
        m_ref[...] = m_new

    @pl.when(j == i)
    def _():
        lam = _diff_lambda(lam_ref, lam_init)
        on = acc_ref[...] / l_ref[...]
        o = on[:tq] - lam * on[tq:]
        o_ref[...] = _diff_out_norm(o, g_ref[...], lam_init)


def _diff_flash(proj, lam, norm_g, lam_init, tq):
    bsz, t, _ = proj.shape
    nq = t // tq
    kq, kk, kv = (BLK512[n] * (BR_W // LANES) for n in ('b_q', 'b_k', 'b_v'))
    return pl.pallas_call(
        functools.partial(_diff_flash_kernel, lam_init=lam_init),
        grid=(bsz, H_B, nq, nq),
        in_specs=[pl.BlockSpec((None, tq, LANES), lambda b, h, i, j: (b, i, kq + h)),
                  pl.BlockSpec((None, tq, LANES), lambda b, h, i, j: (b, jnp.minimum(i, j), kk + h)),
                  pl.BlockSpec((None, tq, LANES), lambda b, h, i, j: (b, jnp.minimum(i, j), kv + h)),
                  pl.BlockSpec((4, DK_B), lambda b, h, i, j: (0, 0)),
                  pl.BlockSpec((1, DV_B), lambda b, h, i, j: (0, 0))],
        out_specs=pl.BlockSpec((None, tq, LANES), lambda b, h, i, j: (b, i, h)),
        out_shape=jax.ShapeDtypeStruct((bsz, t, BR_W), F32),
        scratch_shapes=[pltpu.VMEM((2 * tq, LANES), BF16), pltpu.VMEM((2 * tq, 1), F32),
                        pltpu.VMEM((2 * tq, 1), F32), pltpu.VMEM((2 * tq, DV_B), F32)],
        compiler_params=_cparams("parallel", "parallel", "parallel", "arbitrary"),
    )(proj, proj, proj, lam, norm_g.reshape(1, DV_B))


def _diff_decode_kernel(pt_ref, q_ref, kn_ref, vn_ref, k_ref, v_ref, lam_ref, g_ref, o_ref, m_ref, l_ref, acc_ref,
                        *, lam_init):
    del pt_ref
    pi = pl.program_id(1)
    npg = pl.num_programs(1)
    nrow = 2 * H_B
    q = q_ref[...] * (DK_B ** -0.5)
    row = _iota((nrow, BR_W), 0)
    lane = _iota((nrow, BR_W), 1)
    qm = jnp.where((lane // DK_B) == row, jnp.broadcast_to(q, (nrow, BR_W)), 0.0)

    @pl.when(pi == 0)
    def _():
        m_ref[...] = jnp.full_like(m_ref, NEG_INF)
        l_ref[...] = jnp.zeros_like(l_ref)
        acc_ref[...] = jnp.zeros_like(acc_ref)

    def update(s, v):
        m_old = m_ref[...]
        m_new = jnp.maximum(m_old, jnp.max(s, axis=1, keepdims=True))
        alpha = jnp.exp(m_old - m_new)
        p = jnp.exp(s - m_new)
        l_ref[...] = alpha * l_ref[...] + jnp.sum(p, axis=1, keepdims=True)
        m_ref[...] = m_new
        return alpha, p
    s = _dot_nt(qm.astype(BF16), k_ref[...].astype(BF16))
    alpha, p = update(s, None)
    acc_ref[...] = alpha * acc_ref[...] + _dot(p.astype(BF16), v_ref[...].astype(BF16))

    @pl.when(pi == npg - 1)
    def _():
        s_new = jnp.sum(qm * kn_ref[...], axis=1, keepdims=True)
        alpha2, p2 = update(s_new, None)
        acc = alpha2 * acc_ref[...] + p2 * vn_ref[...]
        on = acc / l_ref[...]
        lam = _diff_lambda(lam_ref, lam_init)
        for h in range(H_B):
            sl = slice(h * DV_B, (h + 1) * DV_B)
            o = on[2 * h:2 * h + 1, sl] - lam * on[2 * h + 1:2 * h + 2, sl]
            o_ref[:, sl] = _diff_out_norm(o, g_ref[...], lam_init)


def _diff_decode(proj, cache_k, cache_v, page_table, lam, norm_g, lam_init, layer):
    n = proj.shape[0]
    npg = page_table.shape[1]

    def col(name):
        return pl.BlockSpec((None, 1, BR_W), lambda b, p, pt: (b, 0, BLK512[name]))
    page = pl.BlockSpec((None, None, PAGE, BR_W), lambda b, p, pt: (layer, pt[b * npg + p], 0, 0))
    grid_spec = pltpu.PrefetchScalarGridSpec(
        num_scalar_prefetch=1,
        grid=(n, npg),
        in_specs=[col('b_q'), col('b_k'), col('b_v'), page, page,
                  pl.BlockSpec((4, DK_B), lambda b, p, pt: (0, 0)),
                  pl.BlockSpec((1, DV_B), lambda b, p, pt: (0, 0))],
        out_specs=pl.BlockSpec((None, 1, BR_W), lambda b, p, pt: (b, 0, 0)),
        scratch_shapes=[pltpu.VMEM((2 * H_B, 1), F32), pltpu.VMEM((2 * H_B, 1), F32),
                        pltpu.VMEM((2 * H_B, BR_W), F32)])
    return pl.pallas_call(
        functools.partial(_diff_decode_kernel, lam_init=lam_init),
        grid_spec=grid_spec,
        out_shape=jax.ShapeDtypeStruct((n, 1, BR_W), F32),
        compiler_params=_cparams("parallel", "arbitrary"),
    )(page_table.reshape(-1), proj, proj, proj, cache_k, cache_v, lam, norm_g.reshape(1, DV_B))


def _cross_kernel(q_ref, k_ref, v_ref, o_ref):
    s = _dot_nt(q_ref[...].astype(BF16), k_ref[...].astype(BF16)) * (HD_X ** -0.5)
    m = jnp.max(s, axis=1, keepdims=True)
    p = jnp.exp(s - m)
    p = p / jnp.sum(p, axis=1, keepdims=True)
    o_ref[...] = _dot(p.astype(BF16), v_ref[...].astype(BF16))


def _cross_prompt(proj, mem_kv, tq):
    bsz, t, _ = proj.shape
    n_mem = mem_kv.shape[1]
    qb = BLK512['x_q'] * (BR_W // LANES)
    return pl.pallas_call(
        _cross_kernel,
        grid=(bsz, H_X, t // tq),
        in_specs=[pl.BlockSpec((None, tq, HD_X), lambda b, h, i: (b, i, qb + h)),
                  pl.BlockSpec((None, n_mem, HD_X), lambda b, h, i: (b, 0, h)),
                  pl.BlockSpec((None, n_mem, HD_X), lambda b, h, i: (b, 0, H_X + h))],
        out_specs=pl.BlockSpec((None, tq, HD_X), lambda b, h, i: (b, i, h)),
        out_shape=jax.ShapeDtypeStruct((bsz, t, BR_W), F32),
        compiler_params=_cparams("parallel", "parallel", "parallel"),
    )(proj, mem_kv, mem_kv)


def _cross_decode_kernel(q_ref, k_ref, v_ref, o_ref):
    q = q_ref[...]
    nrow = SUBLANES
    row = _iota((nrow, BR_W), 0)
    lane = _iota((nrow, BR_W), 1)
    qm = jnp.where((lane // HD_X) == row, jnp.broadcast_to(q, (nrow, BR_W)), 0.0)
    s = _dot_nt(qm.astype(BF16), k_ref[...].astype(BF16)) * (HD_X ** -0.5)
    m = jnp.max(s, axis=1, keepdims=True)
    p = jnp.exp(s - m)
    p = p / jnp.sum(p, axis=1, keepdims=True)
    o = _dot(p.astype(BF16), v_ref[...].astype(BF16))
    for h in range(H_X):
        sl = slice(h * HD_X, (h + 1) * HD_X)
        o_ref[:, sl] = o[h:h + 1, sl]


def _cross_decode(proj, cache_k, cache_v, layer):
    n = proj.shape[0]
    n_mem = cache_k.shape[2]
    kv = pl.BlockSpec((None, None, n_mem, BR_W), lambda b: (layer, b, 0, 0))
    return pl.pallas_call(
        _cross_decode_kernel,
        grid=(n,),
        in_specs=[pl.BlockSpec((None, 1, BR_W), lambda b: (b, 0, BLK512['x_q'])), kv, kv],
        out_specs=pl.BlockSpec((None, 1, BR_W), lambda b: (b, 0, 0)),
        out_shape=jax.ShapeDtypeStruct((n, 1, BR_W), F32),
        compiler_params=_cparams("parallel"),
    )(proj, cache_k, cache_v)


def _merge_kernel(x_ref, oa_ref, az_ref, ob_ref, bz_ref, oc_ref, ox_ref, xz_ref, mg_ref, wb_ref, wo_ref, o_ref):
    branches = (oa_ref[...] * _silu(az_ref[...]), ob_ref[...] * _silu(bz_ref[...]), oc_ref[...],
                ox_ref[...] * _silu(xz_ref[...]))
    acc = None
    for n, br in enumerate(branches):
        gate = _sigmoid(mg_ref[:, n * D_MODEL:(n + 1) * D_MODEL])
        term = gate * _dot(br.astype(BF16), wb_ref[n])
        acc = term if acc is None else acc + term
    o_ref[...] = x_ref[...] + _dot(acc.astype(BF16), wo_ref[...])


def _merge(x, proj, o_a, o_b, o_c, o_x, w_branch, w_out, tm):
    n = x.shape[0]

    def col(name):
        return pl.BlockSpec((tm, BR_W), lambda i: (i, BLK512[name]))
    br = pl.BlockSpec((tm, BR_W), lambda i: (i, 0))
    return pl.pallas_call(
        _merge_kernel,
        grid=(n // tm,),
        in_specs=[pl.BlockSpec((tm, D_MODEL), lambda i: (i, 0)),
                  br, col('a_z'), br, col('b_z'), br, br, col('x_z'),
                  pl.BlockSpec((tm, MERGE_W), lambda i: (i, 0)),
                  pl.BlockSpec((N_BRANCH, BR_W, D_MODEL), lambda i: (0, 0, 0)),
                  pl.BlockSpec((D_MODEL, D_MODEL), lambda i: (0, 0))],
        out_specs=pl.BlockSpec((tm, D_MODEL), lambda i: (i, 0)),
        out_shape=jax.ShapeDtypeStruct((n, D_MODEL), F32),
        compiler_params=_cparams("parallel"),
    )(x, o_a, proj, o_b, proj, o_c, o_x, proj, proj, w_branch, w_out)


def _permute_w_in(w_in):
    d = w_in.shape[0]
    parts = [w_in[..., REF_OFF['merge']:REF_OFF['merge'] + MERGE_W]]
    parts += [w_in[..., REF_OFF[n]:REF_OFF[n] + BR_W] for n in W512]
    parts.append(w_in[..., REF_OFF['a_lora']:REF_OFF['a_lora'] + 2 * LORA])
    parts.append(w_in[..., REF_OFF['c_ab']:REF_OFF['c_ab'] + 2 * H_C])
    parts.append(jnp.zeros(w_in.shape[:-1] + (LANES - 2 * H_C,), w_in.dtype))
    del d
    return jnp.concatenate(parts, axis=-1).astype(BF16)


def _layer_params(p, l):
    mu = p['rwkv_mu'][l]
    row = lambda z: z.reshape(1, -1)
    zeros64 = jnp.zeros((LORA, BR_W), F32)
    rwkv = (row(mu[:BR_W]), row(mu[BR_W:2 * BR_W]), row(mu[2 * BR_W:3 * BR_W]), row(mu[3 * BR_W:]),
            row(p['rwkv_w0'][l]), jnp.concatenate([p['rwkv_w2'][l], zeros64], axis=0),
            row(p['rwkv_a0'][l]), jnp.concatenate([zeros64, p['rwkv_a2'][l]], axis=0),
            row(p['rwkv_kk'][l]), row(p['rwkv_ka'][l]))
    wc = p['gdn_conv_w'][l]
    pad = lambda z: jnp.pad(z, (0, LANES - z.shape[0])).reshape(1, LANES)
    gdn = (wc[:, :BR_W], wc[:, BR_W:2 * BR_W], wc[:, 2 * BR_W:], pad(p['gdn_a_log'][l]), pad(p['gdn_dt_bias'][l]))
    return rwkv, gdn


def _prompt_layer(x, mem, p, wts, l):
    bsz, t, d = x.shape
    n = bsz * t
    rwkv_prm, gdn_prm = _layer_params(p, l)
    lam_init = 0.8 - 0.6 * math.exp(-0.3 * l)
    mem_kv = _norm_matmul(mem.reshape(-1, d), p['norm_mem_g'][l], wts['w_mem'][l], 256, 512)
    mem_kv = mem_kv.reshape(bsz, -1, 2 * BR_W)
    proj = _norm_matmul(x.reshape(n, d), p['norm_g'][l], wts['w_in'][l], min(512, n), 1152)
    proj3 = proj.reshape(bsz, t, PROJ_W)
    tm = min(256, t)
    r, w, k, v, a, b = _rwkv_prep_prompt(proj3, rwkv_prm, tm)
    row = lambda z: z.reshape(1, -1)
    o_a, s_rwkv = _rwkv_chunk(r, w, k, v, a, b, row(p['rwkv_ln_g'][l]), row(p['rwkv_ln_b'][l]), row(p['rwkv_rk'][l]))
    o_b = _diff_flash(proj3, p['diff_lam'][l], p['diff_norm_g'][l], lam_init, min(256, t))
    gq, gk, gv, gb = _gdn_prep_prompt(proj3, gdn_prm, tm)
    o_c, s_gdn = _gdn_chunk(gq, gk, gv, gb, proj3, p['gdn_norm_g'][l])
    o_x = _cross_prompt(proj3, mem_kv, min(512, t))
    flat = lambda z: z.reshape(n, BR_W)
    x_new = _merge(x.reshape(n, d), proj, flat(o_a), flat(o_b), flat(o_c), flat(o_x), wts['w_branch'][l],
                   wts['w_out'][l], min(256, n))

    def cols(name, width=BR_W):
        off = BLK512[name] * BR_W
        return proj3[:, :, off:off + width]
    k_new = cols('b_k').reshape(bsz, t, H_B, 2 * DK_B)
    v_new = cols('b_v').reshape(bsz, t, H_B, DV_B)
    last = proj3[:, t - 1:, :]
    a_off = BLK512['a_r'] * BR_W
    shift = jnp.concatenate([last[:, :, a_off:a_off + 3 * BR_W],
                             last[:, :, LORA_BLK * LANES:LORA_BLK * LANES + 2 * LORA]], axis=-1)
    c_off = BLK512['c_q'] * BR_W
    conv = proj3[:, t - (CONV_W - 1):, c_off:c_off + 3 * BR_W]
    mk = mem_kv[:, :, :BR_W].reshape(bsz, -1, H_X, HD_X)
    mv = mem_kv[:, :, BR_W:].reshape(bsz, -1, H_X, HD_X)
    return x_new.reshape(bsz, t, d), (k_new, v_new, s_rwkv, shift, s_gdn, conv, mk, mv)


def _sample_layer(x, caches, p, wts, l):
    n, _, d = x.shape
    (cache_k, cache_v, page_table, cache_mk, cache_mv, st_rwkv, st_shift, st_gdn, st_conv) = caches
    rwkv_prm, gdn_prm = _layer_params(p, l)
    lam_init = 0.8 - 0.6 * math.exp(-0.3 * l)
    proj = _norm_matmul(x.reshape(n, d), p['norm_g'][l], wts['w_in'][l], n, 1152)
    proj3 = proj.reshape(n, 1, PROJ_W)
    sh = st_shift[l].reshape(n, -1)
    shift_in = (sh[:, :BR_W], sh[:, BR_W:2 * BR_W], sh[:, 2 * BR_W:3 * BR_W], sh[:, 3 * BR_W:])
    rows = _rwkv_prep_sample(proj, shift_in, rwkv_prm)
    o_a, s_rwkv = _rwkv_step(rows, p['rwkv_ln_g'][l], p['rwkv_ln_b'][l], p['rwkv_rk'][l], st_rwkv, l, 8)
    o_b = _diff_decode(proj3, cache_k, cache_v, page_table, p['diff_lam'][l], p['diff_norm_g'][l], lam_init, l)
    buf = jnp.moveaxis(st_conv[l], 1, 0)
    bufs = (buf[:, :, :BR_W], buf[:, :, BR_W:2 * BR_W], buf[:, :, 2 * BR_W:])
    gq, gk, gv, eg, beta = _gdn_prep_sample(proj, bufs, gdn_prm)
    z_off = BLK512['c_z'] * BR_W
    o_c, s_gdn = _gdn_step(gq, gk, gv, eg, beta, proj[:, z_off:z_off + BR_W], p['gdn_norm_g'][l], st_gdn, l, 4)
    o_x = _cross_decode(proj3, cache_mk, cache_mv, l)
    x_new = _merge(x.reshape(n, d), proj, o_a, o_b.reshape(n, BR_W), o_c, o_x.reshape(n, BR_W), wts['w_branch'][l],
                   wts['w_out'][l], n)

    def cols(name, width=BR_W):
        off = BLK512[name] * BR_W
        return proj[:, off:off + width]
    k_new = cols('b_k').reshape(n, 1, H_B, 2 * DK_B)
    v_new = cols('b_v').reshape(n, 1, H_B, DV_B)
    shift = jnp.concatenate([cols('a_r', 3 * BR_W), proj[:, LORA_BLK * LANES:LORA_BLK * LANES + 2 * LORA]],
                            axis=-1).reshape(n, 1, -1)
    conv = jnp.concatenate([st_conv[l][:, 1:], cols('c_q', 3 * BR_W).reshape(n, 1, -1)], axis=1)
    return x_new.reshape(n, 1, d), (k_new, v_new, s_rwkv, shift, s_gdn, conv)


def kernel(x_prompt, x_sample, mem_prompt, cache_diff_k, cache_diff_v, page_table, cache_mem_k, cache_mem_v,
           state_rwkv, state_rwkv_shift, state_gdn, state_gdn_conv, norm_g, w_in, rwkv_mu, rwkv_w0, rwkv_w2,
           rwkv_a0, rwkv_a2, rwkv_kk, rwkv_ka, rwkv_rk, rwkv_ln_g, rwkv_ln_b, diff_lam, diff_norm_g, gdn_conv_w,
           gdn_a_log, gdn_dt_bias, gdn_norm_g, norm_mem_g, w_mem_kv, w_branch, w_out, final_norm_g):
    p = {'norm_g': norm_g, 'rwkv_mu': rwkv_mu, 'rwkv_w0': rwkv_w0, 'rwkv_w2': rwkv_w2, 'rwkv_a0': rwkv_a0,
         'rwkv_a2': rwkv_a2, 'rwkv_kk': rwkv_kk, 'rwkv_ka': rwkv_ka, 'rwkv_rk': rwkv_rk, 'rwkv_ln_g': rwkv_ln_g,
         'rwkv_ln_b': rwkv_ln_b, 'diff_lam': diff_lam, 'diff_norm_g': diff_norm_g, 'gdn_conv_w': gdn_conv_w,
         'gdn_a_log': gdn_a_log, 'gdn_dt_bias': gdn_dt_bias, 'gdn_norm_g': gdn_norm_g, 'norm_mem_g': norm_mem_g}
    depth = w_in.shape[0]
    wts = {'w_in': _permute_w_in(w_in), 'w_mem': w_mem_kv.astype(BF16), 'w_branch': w_branch.astype(BF16),
           'w_out': w_out.astype(BF16)}
    n_s = x_sample.shape[0]
    caches = (cache_diff_k.reshape(*cache_diff_k.shape[:3], BR_W), cache_diff_v.reshape(*cache_diff_v.shape[:3], BR_W),
              page_table, cache_mem_k.reshape(*cache_mem_k.shape[:3], BR_W),
              cache_mem_v.reshape(*cache_mem_v.shape[:3], BR_W), state_rwkv, state_rwkv_shift, state_gdn,
              state_gdn_conv)
    xp, xs = x_prompt, x_sample
    outs_p, outs_s = [], []
    for l in range(depth):
        xp, st_p = _prompt_layer(xp, mem_prompt, p, wts, l)
        xs, st_s = _sample_layer(xs, caches, p, wts, l)
        outs_p.append(st_p)
        outs_s.append(st_s)
    bsz, t, d = xp.shape
    y_prompt = _final_norm(xp.reshape(bsz * t, d), final_norm_g, min(512, bsz * t)).reshape(bsz, t, d)
    y_sample = _final_norm(xs.reshape(n_s, d), final_norm_g, n_s).reshape(n_s, 1, d)
    sp = lambda i: jnp.stack([o[i] for o in outs_p])
    ss = lambda i: jnp.stack([o[i] for o in outs_s])
    return (y_prompt, y_sample, sp(0), sp(1), ss(0), ss(1), sp(6), sp(7), sp(2), ss(2), sp(3), ss(3),
            sp(4), ss(4), sp(5), ss(5))
```

```python
import functools
import math

import jax
import jax.numpy as jnp
from jax import lax
from jax.experimental import pallas as pl
from jax.experimental.pallas import tpu as pltpu

F32 = jnp.float32
BF16 = jnp.bfloat16
HI = lax.Precision.HIGHEST

D_MODEL = 1024
BR_W = D_MODEL // 2
N_BRANCH = 4
N_A = 64
H_A = BR_W // N_A
LORA = 64
H_B = 4
DK_B = 64
DV_B = 128
H_C = 4
DK_C = 128
DV_C = 128
CONV_W = 4
CHUNK = 64
H_X = 4
HD_X = BR_W // H_X
PAGE = 128
RWKV_GN_EPS = 64e-5
NORM_EPS = 1e-6
NEG_INF = -1e30
LANES = 128
SUBLANES = 8
VMEM_LIMIT = 48 * 1024 * 1024
RWKV_NCH = 2
GDN_NCH = 2
FLASH_TQ = 512
FLASH_COLS = 256

W512 = ('a_r', 'a_k', 'a_v', 'a_z', 'b_q', 'b_k', 'b_v', 'b_z', 'c_q', 'c_k', 'c_v', 'c_z', 'x_q', 'x_z')
MERGE_W = N_BRANCH * D_MODEL
BLK512 = {name: MERGE_W // BR_W + i for i, name in enumerate(W512)}
LORA_BLK = (MERGE_W + len(W512) * BR_W) // LANES
CAB_BLK = LORA_BLK + 1
PROJ_W = (CAB_BLK + 1) * LANES
REF_OFF = {'a_r': 0, 'a_k': 512, 'a_v': 1024, 'a_lora': 1536, 'a_z': 1664, 'b_q': 2176, 'b_k': 2688, 'b_v': 3200,
           'b_z': 3712, 'c_q': 4224, 'c_k': 4736, 'c_v': 5248, 'c_ab': 5760, 'c_z': 5768, 'x_q': 6280, 'x_z': 6792,
           'merge': 7304}


def _cparams(*sem):
    return pltpu.CompilerParams(dimension_semantics=sem, vmem_limit_bytes=VMEM_LIMIT)


def _sigmoid(x):
    return 1.0 / (1.0 + jnp.exp(-x))


def _silu(x):
    return x * _sigmoid(x)


def _softplus(x):
    return jnp.maximum(x, 0.0) + jnp.log(1.0 + jnp.exp(-jnp.abs(x)))


def _dot(a, b, precision=None):
    return jnp.dot(a, b, preferred_element_type=F32, precision=precision)


def _dot_nt(a, b, precision=None):
    return lax.dot_general(a, b, (((1,), (1,)), ((), ())), preferred_element_type=F32, precision=precision)


def _iota(shape, axis):
    return lax.broadcasted_iota(jnp.int32, shape, axis)


def _seg_allsum64(x):
    ax = x.ndim - 1
    n = x.shape[ax]
    lane = _iota(x.shape, ax)
    for s in (32, 16, 8, 4, 2, 1):
        up = pltpu.roll(x, s, ax)
        dn = pltpu.roll(x, n - s, ax)
        x = x + jnp.where((lane & s) != 0, up, dn)
    return x


def _split(x):
    hi = x.astype(BF16)
    return hi, (x - hi.astype(F32)).astype(BF16)


def _lhs3(x):
    hi, lo = _split(x)
    return jnp.concatenate([hi, hi, lo], axis=1)


def _rhs3(x):
    hi, lo = _split(x)
    return jnp.concatenate([hi, lo, hi], axis=0)


def _bdot(a, b):
    return _dot(a.astype(BF16), b.astype(BF16))


def _bdot_nt(a, b):
    return _dot_nt(a.astype(BF16), b.astype(BF16))


def _tri_inverse(ls):
    n = ls[0].shape[0]
    eye = (_iota((n, n), 0) == _iota((n, n), 1)).astype(F32)
    xs = [eye + l for l in ls]
    ps = [_dot(_lhs3(l), _rhs3(l)) for l in ls]
    for _ in range(4):
        xps = [_dot(_lhs3(jnp.concatenate([x, p], axis=0)), _rhs3(p)) for x, p in zip(xs, ps)]
        xs = [x + xp[:n] for x, xp in zip(xs, xps)]
        ps = [xp[n:] for xp in xps]
    return [x + _dot(_lhs3(x), _rhs3(p)) for x, p in zip(xs, ps)]


def _norm_matmul_kernel(x_ref, g_ref, w_ref, o_ref, h_ref):
    @pl.when(pl.program_id(1) == 0)
    def _():
        x = x_ref[...]
        y = x * lax.rsqrt(jnp.mean(x * x, axis=-1, keepdims=True) + NORM_EPS)
        h_ref[...] = (y * g_ref[...]).astype(BF16)
    o_ref[...] = _dot(h_ref[...], w_ref[...])


def _norm_matmul(x, g, w, tm, tn):
    n, d = x.shape
    c = w.shape[1]
    return pl.pallas_call(
        _norm_matmul_kernel,
        grid=(n // tm, c // tn),
        in_specs=[pl.BlockSpec((tm, d), lambda i, j: (i, 0)),
                  pl.BlockSpec((1, d), lambda i, j: (0, 0)),
                  pl.BlockSpec((d, tn), lambda i, j: (0, j))],
        out_specs=pl.BlockSpec((tm, tn), lambda i, j: (i, j)),
        out_shape=jax.ShapeDtypeStruct((n, c), F32),
        scratch_shapes=[pltpu.VMEM((tm, d), BF16)],
        compiler_params=_cparams("parallel", "arbitrary"),
    )(x, g.reshape(1, d), w)


def _final_norm_kernel(x_ref, g_ref, o_ref):
    x = x_ref[...]
    o_ref[...] = x * lax.rsqrt(jnp.mean(x * x, axis=-1, keepdims=True) + NORM_EPS) * g_ref[...]


def _final_norm(x, g, tm):
    n, d = x.shape
    return pl.pallas_call(
        _final_norm_kernel,
        grid=(n // tm,),
        in_specs=[pl.BlockSpec((tm, d), lambda i: (i, 0)), pl.BlockSpec((1, d), lambda i: (0, 0))],
        out_specs=pl.BlockSpec((tm, d), lambda i: (i, 0)),
        out_shape=jax.ShapeDtypeStruct((n, d), F32),
        compiler_params=_cparams("parallel"),
    )(x, g.reshape(1, d))


def _rwkv_prep_math(cur, prev, prm):
    mu_r, mu_k, mu_v, mu_l, w0, w2p, a0, a2p, kkp, kap = prm
    r = cur[0] + (prev[0] - cur[0]) * mu_r
    k = cur[1] + (prev[1] - cur[1]) * mu_k
    v = cur[2] + (prev[2] - cur[2]) * mu_v
    lo = cur[3] + (prev[3] - cur[3]) * mu_l
    w_log = -_softplus(-(w0 + _dot(jnp.tanh(lo), w2p))) - 0.5
    log_decay = -jnp.exp(w_log)
    a = _sigmoid(a0 + _dot(lo, a2p))
    kk = k * kkp
    kk = kk * lax.rsqrt(_seg_allsum64(kk * kk) + 1e-6)
    k = k * (1.0 + (a - 1.0) * kap)
    return r, log_decay, k, v, -kk, kk * a


def _rwkv_prep_prompt_kernel(pr, pk, pv, plo, mu_r, mu_k, mu_v, mu_l, w0, w2p, a0, a2p, kkp, kap,
                             o_r, o_w, o_k, o_v, o_a, o_b, c_r, c_k, c_v, c_l):
    carries = (c_r, c_k, c_v, c_l)

    @pl.when(pl.program_id(1) == 0)
    def _():
        for c in carries:
            c[...] = jnp.zeros_like(c)
    cur = (pr[...], pk[...], pv[...], plo[...])
    prev = []
    for p, c in zip(cur, carries):
        row = _iota(p.shape, 0)
        prev.append(jnp.where(row == 0, c[...], pltpu.roll(p, 1, 0)))
    prm = tuple(z[...] for z in (mu_r, mu_k, mu_v, mu_l, w0, w2p, a0, a2p, kkp, kap))
    outs = _rwkv_prep_math(cur, prev, prm)
    for o, val in zip((o_r, o_w, o_k, o_v, o_a, o_b), outs):
        o[...] = val
    tm = cur[0].shape[0]
    for p, c in zip(cur, carries):
        c[...] = p[tm - 1:tm, :]


def _rwkv_prep_sample_kernel(pr, pk, pv, plo, qr, qk, qv, qlo, mu_r, mu_k, mu_v, mu_l, w0, w2p, a0, a2p, kkp, kap,
                             o_r, o_w, o_k, o_v, o_a, o_b):
    cur = (pr[...], pk[...], pv[...], plo[...])
    prev = (qr[...], qk[...], qv[...], qlo[...])
    prm = tuple(z[...] for z in (mu_r, mu_k, mu_v, mu_l, w0, w2p, a0, a2p, kkp, kap))
    outs = _rwkv_prep_math(cur, prev, prm)
    for o, val in zip((o_r, o_w, o_k, o_v, o_a, o_b), outs):
        o[...] = val


def _full_spec(a):
    nd = a.ndim
    return pl.BlockSpec(a.shape, lambda *_: (0,) * nd)


def _rwkv_prep_prompt(proj, prm, tm):
    bsz, t, _ = proj.shape

    def col(blk, w):
        return pl.BlockSpec((None, tm, w), lambda b, i: (b, i, blk))
    out = pl.BlockSpec((None, tm, BR_W), lambda b, i: (b, i, 0))
    return pl.pallas_call(
        _rwkv_prep_prompt_kernel,
        grid=(bsz, t // tm),
        in_specs=[col(BLK512['a_r'], BR_W), col(BLK512['a_k'], BR_W), col(BLK512['a_v'], BR_W), col(LORA_BLK, LANES)]
                 + [_full_spec(p) for p in prm],
        out_specs=[out] * 6,
        out_shape=[jax.ShapeDtypeStruct((bsz, t, BR_W), F32)] * 6,
        scratch_shapes=[pltpu.VMEM((1, BR_W), F32)] * 3 + [pltpu.VMEM((1, LANES), F32)],
        compiler_params=_cparams("parallel", "arbitrary"),
    )(proj, proj, proj, proj, *prm)


def _rwkv_prep_sample(proj, shift, prm):
    n = proj.shape[0]

    def col(blk, w):
        return pl.BlockSpec((n, w), lambda i: (0, blk))
    out = pl.BlockSpec((n, BR_W), lambda i: (0, 0))
    return pl.pallas_call(
        _rwkv_prep_sample_kernel,
        grid=(1,),
        in_specs=[col(BLK512['a_r'], BR_W), col(BLK512['a_k'], BR_W), col(BLK512['a_v'], BR_W), col(LORA_BLK, LANES)]
                 + [_full_spec(s) for s in shift] + [_full_spec(p) for p in prm],
        out_specs=[out] * 6,
        out_shape=[jax.ShapeDtypeStruct((n, BR_W), F32)] * 6,
        compiler_params=_cparams("arbitrary"),
    )(proj, proj, proj, proj, *shift, *prm)


def _stack_pair(x):
    lane = _iota(x.shape, 1)
    return jnp.concatenate([jnp.where(lane < N_A, x, 0.0), jnp.where(lane >= N_A, x, 0.0)], axis=0)


def _rwkv_group_norm_bonus(y, r, k, v, g, b, rk):
    mu = _seg_allsum64(y) * (1.0 / N_A)
    d = y - mu
    var = _seg_allsum64(d * d) * (1.0 / N_A)
    yn = d * lax.rsqrt(var + RWKV_GN_EPS) * g + b
    return yn + _seg_allsum64(r * k * rk) * v


def _rwkv_chunk_kernel(r_ref, w_ref, k_ref, v_ref, a_ref, b_ref, g_ref, bb_ref, rk_ref, o_ref, s_ref, st_ref):
    ci = pl.program_id(1)
    nc = pl.num_programs(1)
    c = CHUNK

    @pl.when(ci == 0)
    def _():
        st_ref[...] = jnp.zeros_like(st_ref)
    rows = w_ref.shape[0]
    lw = w_ref[...]
    ti = _iota((rows, rows), 0)
    tj = _iota((rows, rows), 1)
    tri = ((ti >= tj) & ((ti // c) == (tj // c))).astype(F32)
    cum = _dot(tri, lw, HI)
    p_incl = jnp.exp(cum)
    p_inv = jnp.exp(-cum)
    p_prev = jnp.exp(cum - lw)
    r = r_ref[...]
    k = k_ref[...]
    v = v_ref[...]
    at = a_ref[...] * p_prev
    bt = b_ref[...] * p_inv
    kt = k * p_inv
    rt = r * p_incl
    n2 = 2 * c
    ri = _iota((n2, n2), 0)
    cj = _iota((n2, n2), 1)
    same = (ri >= c) == (cj >= c)
    strict = same & ((ri & (c - 1)) > (cj & (c - 1)))
    incl = same & ((ri & (c - 1)) >= (cj & (c - 1)))
    eye = (ri == cj).astype(F32)
    n_ch = rows // c
    n_pair = H_A // 2
    units = [(ch, p) for ch in range(n_ch) for p in range(n_pair)]

    def rsl(ch):
        return slice(ch * c, (ch + 1) * c)

    def lsl(p):
        return slice(p * LANES, (p + 1) * LANES)
    stk = {name: [_stack_pair(z[rsl(ch), lsl(p)]) for ch, p in units]
           for name, z in (('a', at), ('b', bt), ('k', kt), ('r', rt), ('v', v))}
    prods = [_dot_nt(jnp.concatenate([a_s, r_s], axis=0).astype(BF16),
                     jnp.concatenate([b_s, k_s], axis=0).astype(BF16))
             for a_s, r_s, b_s, k_s in zip(stk['a'], stk['r'], stk['b'], stk['k'])]
    l_ab = [jnp.where(strict, pr[:n2, :n2], 0.0) for pr in prods]
    l_ak = [jnp.where(strict, pr[:n2, n2:], 0.0).astype(BF16) for pr in prods]
    m_r = [jnp.concatenate([jnp.where(incl, pr[n2:, :n2], 0.0), jnp.where(incl, pr[n2:, n2:], 0.0)],
                           axis=1).astype(BF16) for pr in prods]
    v_b = [z.astype(BF16) for z in stk['v']]
    lv = [_dot(l, z) for l, z in zip(l_ak, v_b)]
    t_inv = _tri_inverse(l_ab)
    wu = [_bdot(t, jnp.concatenate([a_s, z], axis=1)) for t, a_s, z in zip(t_inv, stk['a'], lv)]
    p_last = [p_incl[(ch + 1) * c - 1:(ch + 1) * c, lsl(p)] for ch, p in units]
    bk_t = [jnp.concatenate([(b_s * pl_).T, (k_s * pl_).T], axis=1).astype(BF16)
            for b_s, k_s, pl_ in zip(stk['b'], stk['k'], p_last)]
    p_col = [jnp.sum(eye * pl_, axis=1, keepdims=True) for pl_ in p_last]
    r_b = [z.astype(BF16) for z in stk['r']]
    s_t = [st_ref[p] for p in range(n_pair)]
    for ch in range(n_ch):
        ix = [ch * n_pair + p for p in range(n_pair)]
        s_b = [s.astype(BF16) for s in s_t]
        u_b = [(_dot(wu[i][:, :LANES].astype(BF16), s) + wu[i][:, LANES:]).astype(BF16) for i, s in zip(ix, s_b)]
        y_s = [_dot(jnp.concatenate([r_b[i], m_r[i]], axis=1), jnp.concatenate([s, u, v_b[i]], axis=0))
               for i, s, u in zip(ix, s_b, u_b)]
        s_t = [s * p_col[i] + _dot(bk_t[i], jnp.concatenate([u, v_b[i]], axis=0)) for i, s, u in zip(ix, s_t, u_b)]
        for p in range(n_pair):
            y = y_s[p][:c] + y_s[p][c:]
            rs, sl = rsl(ch), lsl(p)
            o_ref[rs, sl] = _rwkv_group_norm_bonus(y, r[rs, sl], k[rs, sl], v[rs, sl], g_ref[:, sl], bb_ref[:, sl],
                                                   rk_ref[:, sl])
    for p in range(n_pair):
        st_ref[p] = s_t[p]

    @pl.when(ci == nc - 1)
    def _():
        for p in range(H_A // 2):
            s_vk = st_ref[p].T
            s_ref[2 * p] = s_vk[:N_A, :N_A]
            s_ref[2 * p + 1] = s_vk[N_A:, N_A:]


def _rwkv_chunk(r, w, k, v, a, b, ln_g, ln_b, rk):
    bsz, t, _ = r.shape
    rows = min(RWKV_NCH * CHUNK, t)
    row = pl.BlockSpec((None, rows, BR_W), lambda bi, ci: (bi, ci, 0))
    par = pl.BlockSpec((1, BR_W), lambda bi, ci: (0, 0))
    return pl.pallas_call(
        _rwkv_chunk_kernel,
        grid=(bsz, t // rows),
        in_specs=[row] * 6 + [par] * 3,
        out_specs=[row, pl.BlockSpec((None, H_A, N_A, N_A), lambda bi, ci: (bi, 0, 0, 0))],
        out_shape=[jax.ShapeDtypeStruct((bsz, t, BR_W), F32), jax.ShapeDtypeStruct((bsz, H_A, N_A, N_A), F32)],
        scratch_shapes=[pltpu.VMEM((H_A // 2, LANES, LANES), F32)],
        compiler_params=_cparams("parallel", "arbitrary"),
    )(r, w, k, v, a, b, ln_g, ln_b, rk)


def _rwkv_step_kernel(r_ref, w_ref, k_ref, v_ref, a_ref, b_ref, g_ref, bb_ref, rk_ref, s_ref, o_ref, so_ref):
    nb = s_ref.shape[0]
    eye = (_iota((N_A, N_A), 0) == _iota((N_A, N_A), 1)).astype(F32)

    def body(bi, carry):
        for h in range(H_A):
            s = s_ref[bi, h]
            r, k, v = r_ref[bi, h], k_ref[bi, h], v_ref[bi, h]
            sa = jnp.sum(s * a_ref[bi, h], axis=1, keepdims=True)
            v_col = jnp.sum(eye * v, axis=1, keepdims=True)
            s2 = s * jnp.exp(w_ref[bi, h]) + sa * b_ref[bi, h] + v_col * k
            y_col = jnp.sum(s2 * r, axis=1, keepdims=True)
            y = jnp.sum(eye * y_col, axis=0, keepdims=True)
            mu = jnp.mean(y, axis=1, keepdims=True)
            d = y - mu
            var = jnp.mean(d * d, axis=1, keepdims=True)
            yn = d * lax.rsqrt(var + RWKV_GN_EPS) * g_ref[h] + bb_ref[h]
            o_ref[bi, h] = yn + jnp.sum(r * k * rk_ref[h], axis=1, keepdims=True) * v
            so_ref[bi, h] = s2
        return carry
    lax.fori_loop(0, nb, body, 0)


def _rwkv_step(rows, ln_g, ln_b, rk, state, layer, nb):
    n = rows[0].shape[0]
    rows = [z.reshape(n, H_A, 1, N_A) for z in rows]
    prm = [z.reshape(H_A, 1, N_A) for z in (ln_g, ln_b, rk)]
    row = pl.BlockSpec((nb, H_A, 1, N_A), lambda i: (i, 0, 0, 0))
    par = pl.BlockSpec((H_A, 1, N_A), lambda i: (0, 0, 0))
    st_in = pl.BlockSpec((None, nb, H_A, N_A, N_A), lambda i: (layer, i, 0, 0, 0))
    st_out = pl.BlockSpec((nb, H_A, N_A, N_A), lambda i: (i, 0, 0, 0))
    o, s_new = pl.pallas_call(
        _rwkv_step_kernel,
        grid=(n // nb,),
        in_specs=[row] * 6 + [par] * 3 + [st_in],
        out_specs=[row, st_out],
        out_shape=[jax.ShapeDtypeStruct((n, H_A, 1, N_A), F32), jax.ShapeDtypeStruct((n, H_A, N_A, N_A), F32)],
        compiler_params=_cparams("parallel"),
    )(*rows, *prm, state)
    return o.reshape(n, BR_W), s_new


def _gdn_norm_qk(q, k):
    qs, ks = [], []
    for h in range(H_C):
        sl = slice(h * DK_C, (h + 1) * DK_C)
        qh, kh = q[:, sl], k[:, sl]
        qs.append(qh * lax.rsqrt(jnp.sum(qh * qh, axis=1, keepdims=True) + 1e-6) * (DK_C ** -0.5))
        ks.append(kh * lax.rsqrt(jnp.sum(kh * kh, axis=1, keepdims=True) + 1e-6))
    return jnp.concatenate(qs, axis=1), jnp.concatenate(ks, axis=1)


def _gdn_gate(ab, alog, dtb):
    lane = _iota(ab.shape, 1)
    return jnp.where(lane < H_C, -jnp.exp(alog) * _softplus(ab + dtb), _sigmoid(ab))


def _gdn_prep_prompt_kernel(xq, xk, xv, xab, wq, wk, wv, alog, dtb, o_q, o_k, o_v, o_gb, t_q, t_k, t_v):
    tails = (t_q, t_k, t_v)

    @pl.when(pl.program_id(1) == 0)
    def _():
        for tl in tails:
            tl[...] = jnp.zeros_like(tl)
    outs = []
    for x_ref, w_ref, tl in zip((xq, xk, xv), (wq, wk, wv), tails):
        x = x_ref[...]
        tm = x.shape[0]
        tail = tl[...]
        acc = x * w_ref[CONV_W - 1:CONV_W, :]
        row8 = _iota(tail.shape, 0)
        for s in range(1, CONV_W):
            rolled = pltpu.roll(x, s, 0)
            head = jnp.where(row8 < s, pltpu.roll(tail, s, 0), rolled[:SUBLANES])
            xs = jnp.concatenate([head, rolled[SUBLANES:]], axis=0)
            acc = acc + xs * w_ref[CONV_W - 1 - s:CONV_W - s, :]
        outs.append(_silu(acc))
        tl[...] = x[tm - SUBLANES:, :]
    q, k = _gdn_norm_qk(outs[0], outs[1])
    o_q[...] = q
    o_k[...] = k
    o_v[...] = outs[2]
    o_gb[...] = _gdn_gate(xab[...], alog[...], dtb[...])


def _gdn_prep_prompt(proj, prm, tm):
    bsz, t, _ = proj.shape

    def col(blk, w):
        return pl.BlockSpec((None, tm, w), lambda b, i: (b, i, blk))
    out = pl.BlockSpec((None, tm, BR_W), lambda b, i: (b, i, 0))
    outg = pl.BlockSpec((None, tm, LANES), lambda b, i: (b, i, 0))
    return pl.pallas_call(
        _gdn_prep_prompt_kernel,
        grid=(bsz, t // tm),
        in_specs=[col(BLK512['c_q'], BR_W), col(BLK512['c_k'], BR_W), col(BLK512['c_v'], BR_W), col(CAB_BLK, LANES)]
                 + [_full_spec(p) for p in prm],
        out_specs=[out, out, out, outg],
        out_shape=[jax.ShapeDtypeStruct((bsz, t, BR_W), F32)] * 3 + [jax.ShapeDtypeStruct((bsz, t, LANES), F32)],
        scratch_shapes=[pltpu.VMEM((SUBLANES, BR_W), F32)] * 3,
        compiler_params=_cparams("parallel", "arbitrary"),
    )(proj, proj, proj, proj, *prm)


def _gdn_prep_sample_kernel(xq, xk, xv, xab, bq, bk, bv, wq, wk, wv, alog, dtb, o_q, o_k, o_v, o_eg, o_beta):
    outs = []
    for x_ref, b_ref, w_ref in zip((xq, xk, xv), (bq, bk, bv), (wq, wk, wv)):
        acc = x_ref[...] * w_ref[CONV_W - 1:CONV_W, :]
        for i in range(CONV_W - 1):
            acc = acc + b_ref[i] * w_ref[i:i + 1, :]
        outs.append(_silu(acc))
    q, k = _gdn_norm_qk(outs[0], outs[1])
    o_q[...] = q
    o_k[...] = k
    o_v[...] = outs[2]
    gb = _gdn_gate(xab[...], alog[...], dtb[...])
    n = gb.shape[0]
    eg = jnp.exp(gb)
    o_eg[...] = jnp.concatenate([jnp.broadcast_to(eg[:, h:h + 1], (n, DV_C)) for h in range(H_C)], axis=1)
    o_beta[...] = jnp.concatenate([jnp.broadcast_to(gb[:, H_C + h:H_C + h + 1], (n, DV_C)) for h in range(H_C)],
                                  axis=1)


def _gdn_prep_sample(proj, bufs, prm):
    n = proj.shape[0]

    def col(blk, w):
        return pl.BlockSpec((n, w), lambda i: (0, blk))
    out = pl.BlockSpec((n, BR_W), lambda i: (0, 0))
    return pl.pallas_call(
        _gdn_prep_sample_kernel,
        grid=(1,),
        in_specs=[col(BLK512['c_q'], BR_W), col(BLK512['c_k'], BR_W), col(BLK512['c_v'], BR_W), col(CAB_BLK, LANES)]
                 + [_full_spec(z) for z in bufs] + [_full_spec(p) for p in prm],
        out_specs=[out] * 5,
        out_shape=[jax.ShapeDtypeStruct((n, BR_W), F32)] * 5,
        compiler_params=_cparams("arbitrary"),
    )(proj, proj, proj, proj, *bufs, *prm)


def _gdn_out_norm(o, g, z):
    return o * lax.rsqrt(jnp.mean(o * o, axis=1, keepdims=True) + NORM_EPS) * g * _silu(z)


def _gdn_chunk_kernel(q_ref, k_ref, v_ref, gb_ref, z_ref, g_ref, o_ref, s_ref, st_ref):
    ci = pl.program_id(1)
    nc = pl.num_programs(1)
    c = CHUNK
    rows = gb_ref.shape[0]
    n_ch = rows // c

    @pl.when(ci == 0)
    def _():
        st_ref[...] = jnp.zeros_like(st_ref)
    gb = gb_ref[...]
    ri = _iota((rows, rows), 0)
    cj = _iota((rows, rows), 1)
    same = (ri // c) == (cj // c)
    incl = same & (ri >= cj)
    strict = same & (ri > cj)
    cum = _dot(incl.astype(F32), gb, HI)
    cum_t = cum.T
    heads = range(H_C)
    hsl = [slice(h * DK_C, (h + 1) * DK_C) for h in heads]
    gc_col = [cum[:, h:h + 1] for h in heads]
    beta = [gb[:, H_C + h:H_C + h + 1] for h in heads]
    decay = [jnp.where(incl, jnp.exp(jnp.where(incl, gc_col[h] - cum_t[h:h + 1, :], 0.0)), 0.0) for h in heads]
    q = [q_ref[:, sl] for sl in hsl]
    k = [k_ref[:, sl] for sl in hsl]
    kb = [k[h] * beta[h] for h in heads]
    prod = [_dot_nt(jnp.concatenate([kb[h], q[h]], axis=0).astype(BF16), k[h].astype(BF16)) for h in heads]
    t_inv = _tri_inverse([-jnp.where(strict, prod[h][:rows] * decay[h], 0.0) for h in heads])
    qk = [jnp.where(incl, prod[h][rows:] * decay[h], 0.0) for h in heads]
    uw = [_bdot(t_inv[h], jnp.concatenate([v_ref[:, hsl[h]] * beta[h], kb[h] * jnp.exp(gc_col[h])], axis=1))
          for h in heads]
    qg = [q[h] * jnp.exp(gc_col[h]) for h in heads]
    s = [st_ref[h] for h in heads]
    for ch in range(n_ch):
        rs = slice(ch * c, (ch + 1) * c)
        g_last = [cum[(ch + 1) * c - 1:(ch + 1) * c, h:h + 1] for h in heads]
        s_b = [z.astype(BF16) for z in s]
        v_nb = [(uw[h][rs, :DV_C] - _dot(uw[h][rs, DV_C:].astype(BF16), s_b[h])).astype(BF16) for h in heads]
        zero = jnp.zeros_like(v_nb[0])
        o = [_dot(jnp.concatenate([qg[h][rs], qk[h][rs]], axis=1).astype(BF16),
                  jnp.concatenate([s_b[h]] + [v_nb[h] if i == ch else zero for i in range(n_ch)], axis=0))
             for h in heads]
        s = [s[h] * jnp.exp(g_last[h]) + _dot((k[h][rs] * jnp.exp(g_last[h] - gc_col[h][rs])).T.astype(BF16), v_nb[h])
             for h in heads]
        for h in heads:
            o_ref[rs, hsl[h]] = _gdn_out_norm(o[h], g_ref[...], z_ref[rs, hsl[h]])
    for h in heads:
        st_ref[h] = s[h]

    @pl.when(ci == nc - 1)
    def _():
        s_ref[...] = st_ref[...]


def _gdn_chunk(q, k, v, gb, proj, norm_g):
    bsz, t, _ = q.shape
    rows = min(GDN_NCH * CHUNK, t)
    row = pl.BlockSpec((None, rows, BR_W), lambda bi, ci: (bi, ci, 0))
    return pl.pallas_call(
        _gdn_chunk_kernel,
        grid=(bsz, t // rows),
        in_specs=[row, row, row,
                  pl.BlockSpec((None, rows, LANES), lambda bi, ci: (bi, ci, 0)),
                  pl.BlockSpec((None, rows, BR_W), lambda bi, ci: (bi, ci, BLK512['c_z'])),
                  pl.BlockSpec((1, DV_C), lambda bi, ci: (0, 0))],
        out_specs=[row, pl.BlockSpec((None, H_C, DK_C, DV_C), lambda bi, ci: (bi, 0, 0, 0))],
        out_shape=[jax.ShapeDtypeStruct((bsz, t, BR_W), F32), jax.ShapeDtypeStruct((bsz, H_C, DK_C, DV_C), F32)],
        scratch_shapes=[pltpu.VMEM((H_C, DK_C, DV_C), F32)],
        compiler_params=_cparams("parallel", "arbitrary"),
    )(q, k, v, gb, proj, norm_g.reshape(1, DV_C))


def _gdn_step_kernel(q_ref, k_ref, v_ref, eg_ref, beta_ref, z_ref, g_ref, s_ref, o_ref, so_ref):
    nb = s_ref.shape[0]
    eye = (_iota((DK_C, DK_C), 0) == _iota((DK_C, DK_C), 1)).astype(F32)

    def body(bi, carry):
        for h in range(H_C):
            s = s_ref[bi, h]
            eg = eg_ref[bi, h]
            k_col = jnp.sum(eye * k_ref[bi, h], axis=1, keepdims=True)
            q_col = jnp.sum(eye * q_ref[bi, h], axis=1, keepdims=True)
            ks = jnp.sum(s * k_col, axis=0, keepdims=True)
            u = beta_ref[bi, h] * (v_ref[bi, h] - eg * ks)
            s2 = s * eg + k_col * u
            o = jnp.sum(s2 * q_col, axis=0, keepdims=True)
            o_ref[bi, h] = _gdn_out_norm(o, g_ref[...], z_ref[bi, h])
            so_ref[bi, h] = s2
        return carry
    lax.fori_loop(0, nb, body, 0)


def _gdn_step(q, k, v, eg, beta, z, norm_g, state, layer, nb):
    n = q.shape[0]
    rows = [x.reshape(n, H_C, 1, DV_C) for x in (q, k, v, eg, beta, z)]
    row = pl.BlockSpec((nb, H_C, 1, DV_C), lambda i: (i, 0, 0, 0))
    st_in = pl.BlockSpec((None, nb, H_C, DK_C, DV_C), lambda i: (layer, i, 0, 0, 0))
    st_out = pl.BlockSpec((nb, H_C, DK_C, DV_C), lambda i: (i, 0, 0, 0))
    o, s_new = pl.pallas_call(
        _gdn_step_kernel,
        grid=(n // nb,),
        in_specs=[row] * 6 + [pl.BlockSpec((1, DV_C), lambda i: (0, 0)), st_in],
        out_specs=[row, st_out],
        out_shape=[jax.ShapeDtypeStruct((n, H_C, 1, DV_C), F32), jax.ShapeDtypeStruct((n, H_C, DK_C, DV_C), F32)],
        compiler_params=_cparams("parallel"),
    )(*rows, norm_g.reshape(1, DV_C), state)
    return o.reshape(n, BR_W), s_new


def _diff_lambda(lam_ref, lam_init):
    lv = lam_ref[...]
    s1 = jnp.sum(lv[0:1] * lv[1:2], axis=1, keepdims=True)
    s2 = jnp.sum(lv[2:3] * lv[3:4], axis=1, keepdims=True)
    return jnp.exp(s1) - jnp.exp(s2) + lam_init


def _diff_out_norm(o, g, lam_init):
    return o * lax.rsqrt(jnp.mean(o * o, axis=1, keepdims=True) + 1e-5) * g * (1.0 - lam_init)


def _diff_flash_kernel(q_ref, k_ref, v_ref, lam_ref, g_ref, o_ref, qs_ref, m_ref, l_ref, acc_ref, s_ref, *, lam_init):
    i = pl.program_id(2)
    tq = q_ref.shape[0]
    q = q_ref[...] * (DK_B ** -0.5)
    lane = _iota(q.shape, 1)
    qs_ref[:tq] = jnp.where(lane < DK_B, q, 0.0).astype(BF16)
    qs_ref[tq:] = jnp.where(lane >= DK_B, q, 0.0).astype(BF16)
    m_ref[...] = jnp.full_like(m_ref, NEG_INF)
    l_ref[...] = jnp.zeros_like(l_ref)
    acc_ref[...] = jnp.zeros_like(acc_ref)

    n_grp = 2 * tq // FLASH_COLS
    grp = [slice(g * FLASH_COLS, (g + 1) * FLASH_COLS) for g in range(n_grp)]

    def kv_rows(j):
        return pl.ds(pl.multiple_of(j * tq, tq), tq)

    def scores(j):
        s_ref[j % 2] = _dot_nt(k_ref[kv_rows(j), :].astype(BF16), qs_ref[...])

    def consume(j, diagonal):
        v_t = v_ref[kv_rows(j), :].T.astype(BF16)
        slot = j % 2
        for g, sl in enumerate(grp):
            s = s_ref[slot, :, sl]
            if diagonal:
                s = jnp.where(_iota(s.shape, 0) <= ((_iota(s.shape, 1) + g * FLASH_COLS) & (tq - 1)), s, NEG_INF)
            m_old = m_ref[:, sl]
            m_new = jnp.maximum(m_old, jnp.max(s, axis=0, keepdims=True))
            alpha = jnp.exp(m_old - m_new)
            p = jnp.exp(s - m_new)
            l_ref[:, sl] = alpha * l_ref[:, sl] + jnp.sum(p, axis=0, keepdims=True)
            acc_ref[:, sl] = alpha * acc_ref[:, sl] + _dot(v_t, p.astype(BF16))
            m_ref[:, sl] = m_new

    scores(0)

    def body(j, carry):
        scores(j)
        consume(j - 1, False)
        return carry
    lax.fori_loop(1, i + 1, body, 0)
    consume(i, True)
    lam = _diff_lambda(lam_ref, lam_init)
    on = acc_ref[...] / l_ref[...]
    o = (on[:, :tq] - lam * on[:, tq:]).T
    o_ref[...] = _diff_out_norm(o, g_ref[...], lam_init)


def _diff_flash(proj, lam, norm_g, lam_init, tq):
    bsz, t, _ = proj.shape
    kq, kk, kv = (BLK512[n] * (BR_W // LANES) for n in ('b_q', 'b_k', 'b_v'))
    return pl.pallas_call(
        functools.partial(_diff_flash_kernel, lam_init=lam_init),
        grid=(bsz, H_B, t // tq),
        in_specs=[pl.BlockSpec((None, tq, LANES), lambda b, h, i: (b, i, kq + h)),
                  pl.BlockSpec((None, t, LANES), lambda b, h, i: (b, 0, kk + h)),
                  pl.BlockSpec((None, t, LANES), lambda b, h, i: (b, 0, kv + h)),
                  pl.BlockSpec((4, DK_B), lambda b, h, i: (0, 0)),
                  pl.BlockSpec((1, DV_B), lambda b, h, i: (0, 0))],
        out_specs=pl.BlockSpec((None, tq, LANES), lambda b, h, i: (b, i, h)),
        out_shape=jax.ShapeDtypeStruct((bsz, t, BR_W), F32),
        scratch_shapes=[pltpu.VMEM((2 * tq, LANES), BF16), pltpu.VMEM((1, 2 * tq), F32),
                        pltpu.VMEM((1, 2 * tq), F32), pltpu.VMEM((DV_B, 2 * tq), F32),
                        pltpu.VMEM((2, tq, 2 * tq), F32)],
        compiler_params=_cparams("parallel", "parallel", "arbitrary"),
    )(proj, proj, proj, lam, norm_g.reshape(1, DV_B))


def _head_rows(x, n_rep):
    rows = []
    for h in range(x.shape[1] // LANES):
        rows += [x[:, h * LANES:(h + 1) * LANES]] * n_rep
    return jnp.concatenate(rows, axis=0)


def _diff_decode_kernel(pt_ref, q_ref, kn_ref, vn_ref, *rest, lam_init, npg):
    del pt_ref
    k_refs, v_refs = rest[:npg], rest[npg:2 * npg]
    lam_ref, g_ref, o_ref = rest[2 * npg:]
    nrow = 2 * H_B
    q8 = _head_rows(q_ref[...] * (DK_B ** -0.5), 2)
    lane = _iota(q8.shape, 1)
    row = _iota(q8.shape, 0)
    q8 = jnp.where((lane >= DK_B) == ((row & 1) == 1), q8, 0.0)
    q8b = q8.astype(BF16)
    ncol = k_refs[0].shape[0]
    valid = (_iota((nrow, ncol), 1) & (H_B - 1)) == (_iota((nrow, ncol), 0) >> 1)
    s_new = jnp.sum(q8 * _head_rows(kn_ref[...], 2), axis=1, keepdims=True)
    s_pages = [jnp.where(valid, _dot_nt(q8b, k[...].astype(BF16)), NEG_INF) for k in k_refs]
    m = s_new
    for s in s_pages:
        m = jnp.maximum(m, jnp.max(s, axis=1, keepdims=True))
    p_new = jnp.exp(s_new - m)
    l_sum = p_new
    acc = p_new * _head_rows(vn_ref[...], 2)
    for s, v in zip(s_pages, v_refs):
        p = jnp.exp(s - m)
        l_sum = l_sum + jnp.sum(p, axis=1, keepdims=True)
        acc = acc + _dot(p.astype(BF16), v[...].astype(BF16))
    on = acc / l_sum
    lam = _diff_lambda(lam_ref, lam_init)
    for h in range(H_B):
        o = on[2 * h:2 * h + 1] - lam * on[2 * h + 1:2 * h + 2]
        o_ref[:, h * DV_B:(h + 1) * DV_B] = _diff_out_norm(o, g_ref[...], lam_init)


def _diff_decode(proj, cache_k, cache_v, page_table, lam, norm_g, lam_init, layer):
    n = proj.shape[0]
    npg = page_table.shape[1]

    def col(name):
        return pl.BlockSpec((None, 1, BR_W), lambda b, pt: (b, 0, BLK512[name]))

    def page(i):
        return pl.BlockSpec((None, None, PAGE * H_B, DV_B), lambda b, pt: (layer, pt[b * npg + i], 0, 0))
    pages = [page(i) for i in range(npg)]
    grid_spec = pltpu.PrefetchScalarGridSpec(
        num_scalar_prefetch=1,
        grid=(n,),
        in_specs=[col('b_q'), col('b_k'), col('b_v')] + pages + pages
                 + [pl.BlockSpec((4, DK_B), lambda b, pt: (0, 0)), pl.BlockSpec((1, DV_B), lambda b, pt: (0, 0))],
        out_specs=pl.BlockSpec((None, 1, BR_W), lambda b, pt: (b, 0, 0)))
    return pl.pallas_call(
        functools.partial(_diff_decode_kernel, lam_init=lam_init, npg=npg),
        grid_spec=grid_spec,
        out_shape=jax.ShapeDtypeStruct((n, 1, BR_W), F32),
        compiler_params=_cparams("parallel"),
    )(page_table.reshape(-1), proj, proj, proj, *([cache_k] * npg), *([cache_v] * npg), lam,
      norm_g.reshape(1, DV_B))


def _cross_kernel(q_ref, k_ref, v_ref, o_ref):
    s = _dot_nt(q_ref[...].astype(BF16), k_ref[...].astype(BF16)) * (HD_X ** -0.5)
    m = jnp.max(s, axis=1, keepdims=True)
    p = jnp.exp(s - m)
    p = p / jnp.sum(p, axis=1, keepdims=True)
    o_ref[...] = _dot(p.astype(BF16), v_ref[...].astype(BF16))


def _cross_prompt(proj, mem_kv, tq):
    bsz, t, _ = proj.shape
    n_mem = mem_kv.shape[1]
    qb = BLK512['x_q'] * (BR_W // LANES)
    return pl.pallas_call(
        _cross_kernel,
        grid=(bsz, H_X, t // tq),
        in_specs=[pl.BlockSpec((None, tq, HD_X), lambda b, h, i: (b, i, qb + h)),
                  pl.BlockSpec((None, n_mem, HD_X), lambda b, h, i: (b, 0, h)),
                  pl.BlockSpec((None, n_mem, HD_X), lambda b, h, i: (b, 0, H_X + h))],
        out_specs=pl.BlockSpec((None, tq, HD_X), lambda b, h, i: (b, i, h)),
        out_shape=jax.ShapeDtypeStruct((bsz, t, BR_W), F32),
        compiler_params=_cparams("parallel", "parallel", "parallel"),
    )(proj, mem_kv, mem_kv)


def _cross_decode_kernel(q_ref, k_ref, v_ref, o_ref):
    nb = q_ref.shape[0]
    ncol = k_ref.shape[1]
    valid = (_iota((SUBLANES, ncol), 1) & (H_X - 1)) == _iota((SUBLANES, ncol), 0)
    for b in range(nb):
        q8 = jnp.concatenate([_head_rows(q_ref[b], 1), jnp.zeros((SUBLANES - H_X, HD_X), F32)], axis=0)
        s = _dot_nt(q8.astype(BF16), k_ref[b].astype(BF16)) * (HD_X ** -0.5)
        s = jnp.where(valid, s, NEG_INF)
        m = jnp.max(s, axis=1, keepdims=True)
        p = jnp.exp(s - m)
        o = _dot(p.astype(BF16), v_ref[b].astype(BF16)) / jnp.sum(p, axis=1, keepdims=True)
        for h in range(H_X):
            o_ref[b, :, h * HD_X:(h + 1) * HD_X] = o[h:h + 1]


def _cross_decode(proj, cache_k, cache_v, layer, nb):
    n = proj.shape[0]
    rows = cache_k.shape[2]
    kv = pl.BlockSpec((None, nb, rows, HD_X), lambda i: (layer, i, 0, 0))
    return pl.pallas_call(
        _cross_decode_kernel,
        grid=(n // nb,),
        in_specs=[pl.BlockSpec((nb, 1, BR_W), lambda i: (i, 0, BLK512['x_q'])), kv, kv],
        out_specs=pl.BlockSpec((nb, 1, BR_W), lambda i: (i, 0, 0)),
        out_shape=jax.ShapeDtypeStruct((n, 1, BR_W), F32),
        compiler_params=_cparams("parallel"),
    )(proj, cache_k, cache_v)


def _merge_kernel(x_ref, oa_ref, az_ref, ob_ref, bz_ref, oc_ref, ox_ref, xz_ref, mg_ref, wb_ref, wo_ref, o_ref):
    branches = (oa_ref[...] * _silu(az_ref[...]), ob_ref[...] * _silu(bz_ref[...]), oc_ref[...],
                ox_ref[...] * _silu(xz_ref[...]))
    acc = None
    for n, br in enumerate(branches):
        gate = _sigmoid(mg_ref[:, n * D_MODEL:(n + 1) * D_MODEL])
        term = gate * _dot(br.astype(BF16), wb_ref[n])
        acc = term if acc is None else acc + term
    o_ref[...] = x_ref[...] + _dot(acc.astype(BF16), wo_ref[...])


def _merge(x, proj, o_a, o_b, o_c, o_x, w_branch, w_out, tm):
    n = x.shape[0]

    def col(name):
        return pl.BlockSpec((tm, BR_W), lambda i: (i, BLK512[name]))
    br = pl.BlockSpec((tm, BR_W), lambda i: (i, 0))
    return pl.pallas_call(
        _merge_kernel,
        grid=(n // tm,),
        in_specs=[pl.BlockSpec((tm, D_MODEL), lambda i: (i, 0)),
                  br, col('a_z'), br, col('b_z'), br, br, col('x_z'),
                  pl.BlockSpec((tm, MERGE_W), lambda i: (i, 0)),
                  pl.BlockSpec((N_BRANCH, BR_W, D_MODEL), lambda i: (0, 0, 0)),
                  pl.BlockSpec((D_MODEL, D_MODEL), lambda i: (0, 0))],
        out_specs=pl.BlockSpec((tm, D_MODEL), lambda i: (i, 0)),
        out_shape=jax.ShapeDtypeStruct((n, D_MODEL), F32),
        compiler_params=_cparams("parallel"),
    )(x, o_a, proj, o_b, proj, o_c, o_x, proj, proj, w_branch, w_out)


def _merge_token_head(cache):
    return cache.reshape(*cache.shape[:-3], cache.shape[-3] * cache.shape[-2], cache.shape[-1])


def _permute_w_in(w_in):
    d = w_in.shape[0]
    parts = [w_in[..., REF_OFF['merge']:REF_OFF['merge'] + MERGE_W]]
    parts += [w_in[..., REF_OFF[n]:REF_OFF[n] + BR_W] for n in W512]
    parts.append(w_in[..., REF_OFF['a_lora']:REF_OFF['a_lora'] + 2 * LORA])
    parts.append(w_in[..., REF_OFF['c_ab']:REF_OFF['c_ab'] + 2 * H_C])
    parts.append(jnp.zeros(w_in.shape[:-1] + (LANES - 2 * H_C,), w_in.dtype))
    del d
    return jnp.concatenate(parts, axis=-1).astype(BF16)


def _layer_params(p, l):
    mu = p['rwkv_mu'][l]
    row = lambda z: z.reshape(1, -1)
    zeros64 = jnp.zeros((LORA, BR_W), F32)
    rwkv = (row(mu[:BR_W]), row(mu[BR_W:2 * BR_W]), row(mu[2 * BR_W:3 * BR_W]), row(mu[3 * BR_W:]),
            row(p['rwkv_w0'][l]), jnp.concatenate([p['rwkv_w2'][l], zeros64], axis=0),
            row(p['rwkv_a0'][l]), jnp.concatenate([zeros64, p['rwkv_a2'][l]], axis=0),
            row(p['rwkv_kk'][l]), row(p['rwkv_ka'][l]))
    wc = p['gdn_conv_w'][l]
    pad = lambda z: jnp.pad(z, (0, LANES - z.shape[0])).reshape(1, LANES)
    gdn = (wc[:, :BR_W], wc[:, BR_W:2 * BR_W], wc[:, 2 * BR_W:], pad(p['gdn_a_log'][l]), pad(p['gdn_dt_bias'][l]))
    return rwkv, gdn


def _prompt_layer(x, mem, p, wts, l):
    bsz, t, d = x.shape
    n = bsz * t
    rwkv_prm, gdn_prm = _layer_params(p, l)
    lam_init = 0.8 - 0.6 * math.exp(-0.3 * l)
    mem_kv = _norm_matmul(mem.reshape(-1, d), p['norm_mem_g'][l], wts['w_mem'][l], 256, 512)
    mem_kv = mem_kv.reshape(bsz, -1, 2 * BR_W)
    proj = _norm_matmul(x.reshape(n, d), p['norm_g'][l], wts['w_in'][l], min(512, n), 1152)
    proj3 = proj.reshape(bsz, t, PROJ_W)
    tm = min(256, t)
    r, w, k, v, a, b = _rwkv_prep_prompt(proj3, rwkv_prm, tm)
    row = lambda z: z.reshape(1, -1)
    o_a, s_rwkv = _rwkv_chunk(r, w, k, v, a, b, row(p['rwkv_ln_g'][l]), row(p['rwkv_ln_b'][l]), row(p['rwkv_rk'][l]))
    o_b = _diff_flash(proj3, p['diff_lam'][l], p['diff_norm_g'][l], lam_init, min(FLASH_TQ, t // 2))
    gq, gk, gv, gb = _gdn_prep_prompt(proj3, gdn_prm, tm)
    o_c, s_gdn = _gdn_chunk(gq, gk, gv, gb, proj3, p['gdn_norm_g'][l])
    o_x = _cross_prompt(proj3, mem_kv, min(512, t))
    flat = lambda z: z.reshape(n, BR_W)
    x_new = _merge(x.reshape(n, d), proj, flat(o_a), flat(o_b), flat(o_c), flat(o_x), wts['w_branch'][l],
                   wts['w_out'][l], min(256, n))

    def cols(name, width=BR_W):
        off = BLK512[name] * BR_W
        return proj3[:, :, off:off + width]
    k_new = cols('b_k').reshape(bsz, t, H_B, 2 * DK_B)
    v_new = cols('b_v').reshape(bsz, t, H_B, DV_B)
    last = proj3[:, t - 1:, :]
    a_off = BLK512['a_r'] * BR_W
    shift = jnp.concatenate([last[:, :, a_off:a_off + 3 * BR_W],
                             last[:, :, LORA_BLK * LANES:LORA_BLK * LANES + 2 * LORA]], axis=-1)
    c_off = BLK512['c_q'] * BR_W
    conv = proj3[:, t - (CONV_W - 1):, c_off:c_off + 3 * BR_W]
    mk = mem_kv[:, :, :BR_W].reshape(bsz, -1, H_X, HD_X)
    mv = mem_kv[:, :, BR_W:].reshape(bsz, -1, H_X, HD_X)
    return x_new.reshape(bsz, t, d), (k_new, v_new, s_rwkv, shift, s_gdn, conv, mk, mv)


def _sample_layer(x, caches, p, wts, l):
    n, _, d = x.shape
    (cache_k, cache_v, page_table, cache_mk, cache_mv, st_rwkv, st_shift, st_gdn, st_conv) = caches
    rwkv_prm, gdn_prm = _layer_params(p, l)
    lam_init = 0.8 - 0.6 * math.exp(-0.3 * l)
    proj = _norm_matmul(x.reshape(n, d), p['norm_g'][l], wts['w_in'][l], n, 1152)
    proj3 = proj.reshape(n, 1, PROJ_W)
    sh = st_shift[l].reshape(n, -1)
    shift_in = (sh[:, :BR_W], sh[:, BR_W:2 * BR_W], sh[:, 2 * BR_W:3 * BR_W], sh[:, 3 * BR_W:])
    rows = _rwkv_prep_sample(proj, shift_in, rwkv_prm)
    o_a, s_rwkv = _rwkv_step(rows, p['rwkv_ln_g'][l], p['rwkv_ln_b'][l], p['rwkv_rk'][l], st_rwkv, l, 8)
    o_b = _diff_decode(proj3, cache_k, cache_v, page_table, p['diff_lam'][l], p['diff_norm_g'][l], lam_init, l)
    buf = jnp.moveaxis(st_conv[l], 1, 0)
    bufs = (buf[:, :, :BR_W], buf[:, :, BR_W:2 * BR_W], buf[:, :, 2 * BR_W:])
    gq, gk, gv, eg, beta = _gdn_prep_sample(proj, bufs, gdn_prm)
    z_off = BLK512['c_z'] * BR_W
    o_c, s_gdn = _gdn_step(gq, gk, gv, eg, beta, proj[:, z_off:z_off + BR_W], p['gdn_norm_g'][l], st_gdn, l, 4)
    o_x = _cross_decode(proj3, cache_mk, cache_mv, l, 4)
    x_new = _merge(x.reshape(n, d), proj, o_a, o_b.reshape(n, BR_W), o_c, o_x.reshape(n, BR_W), wts['w_branch'][l],
                   wts['w_out'][l], n)

    def cols(name, width=BR_W):
        off = BLK512[name] * BR_W
        return proj[:, off:off + width]
    k_new = cols('b_k').reshape(n, 1, H_B, 2 * DK_B)
    v_new = cols('b_v').reshape(n, 1, H_B, DV_B)
    shift = jnp.concatenate([cols('a_r', 3 * BR_W), proj[:, LORA_BLK * LANES:LORA_BLK * LANES + 2 * LORA]],
                            axis=-1).reshape(n, 1, -1)
    conv = jnp.concatenate([st_conv[l][:, 1:], cols('c_q', 3 * BR_W).reshape(n, 1, -1)], axis=1)
    return x_new.reshape(n, 1, d), (k_new, v_new, s_rwkv, shift, s_gdn, conv)


def kernel(x_prompt, x_sample, mem_prompt, cache_diff_k, cache_diff_v, page_table, cache_mem_k, cache_mem_v,
           state_rwkv, state_rwkv_shift, state_gdn, state_gdn_conv, norm_g, w_in, rwkv_mu, rwkv_w0, rwkv_w2,
           rwkv_a0, rwkv_a2, rwkv_kk, rwkv_ka, rwkv_rk, rwkv_ln_g, rwkv_ln_b, diff_lam, diff_norm_g, gdn_conv_w,
           gdn_a_log, gdn_dt_bias, gdn_norm_g, norm_mem_g, w_mem_kv, w_branch, w_out, final_norm_g):
    p = {'norm_g': norm_g, 'rwkv_mu': rwkv_mu, 'rwkv_w0': rwkv_w0, 'rwkv_w2': rwkv_w2, 'rwkv_a0': rwkv_a0,
         'rwkv_a2': rwkv_a2, 'rwkv_kk': rwkv_kk, 'rwkv_ka': rwkv_ka, 'rwkv_rk': rwkv_rk, 'rwkv_ln_g': rwkv_ln_g,
         'rwkv_ln_b': rwkv_ln_b, 'diff_lam': diff_lam, 'diff_norm_g': diff_norm_g, 'gdn_conv_w': gdn_conv_w,
         'gdn_a_log': gdn_a_log, 'gdn_dt_bias': gdn_dt_bias, 'gdn_norm_g': gdn_norm_g, 'norm_mem_g': norm_mem_g}
    depth = w_in.shape[0]
    wts = {'w_in': _permute_w_in(w_in), 'w_mem': w_mem_kv.astype(BF16), 'w_branch': w_branch.astype(BF16),
           'w_out': w_out.astype(BF16)}
    n_s = x_sample.shape[0]
    caches = (_merge_token_head(cache_diff_k), _merge_token_head(cache_diff_v), page_table,
              _merge_token_head(cache_mem_k), _merge_token_head(cache_mem_v), state_rwkv, state_rwkv_shift, state_gdn,
              state_gdn_conv)
    xp, xs = x_prompt, x_sample
    outs_p, outs_s = [], []
    for l in range(depth):
        xp, st_p = _prompt_layer(xp, mem_prompt, p, wts, l)
        xs, st_s = _sample_layer(xs, caches, p, wts, l)
        outs_p.append(st_p)
        outs_s.append(st_s)
    bsz, t, d = xp.shape
    y_prompt = _final_norm(xp.reshape(bsz * t, d), final_norm_g, min(512, bsz * t)).reshape(bsz, t, d)
    y_sample = _final_norm(xs.reshape(n_s, d), final_norm_g, n_s).reshape(n_s, 1, d)
    sp = lambda i: jnp.stack([o[i] for o in outs_p])
    ss = lambda i: jnp.stack([o[i] for o in outs_s])
    return (y_prompt, y_sample, sp(0), sp(1), ss(0), ss(1), sp(6), sp(7), sp(2), ss(2), sp(3), ss(3),
            sp(4), ss(4), sp(5), ss(5))
```

```python
import functools
import math

import jax
import jax.numpy as jnp
from jax import lax
from jax.experimental import pallas as pl
from jax.experimental.pallas import tpu as pltpu

F32 = jnp.float32
BF16 = jnp.bfloat16
HI = lax.Precision.HIGHEST

D_MODEL = 1024
BR_W = D_MODEL // 2
N_BRANCH = 4
N_A = 64
H_A = BR_W // N_A
LORA = 64
H_B = 4
DK_B = 64
DV_B = 128
H_C = 4
DK_C = 128
DV_C = 128
CONV_W = 4
CHUNK = 64
H_X = 4
HD_X = BR_W // H_X
PAGE = 128
RWKV_GN_EPS = 64e-5
NORM_EPS = 1e-6
NEG_INF = -1e30
LANES = 128
SUBLANES = 8
VMEM_LIMIT = 48 * 1024 * 1024
PROJ_TM = 1024
PROJ_TN = 1280
MERGE_TM = 256
KV_TM = 512
RWKV_NCH = 2
GDN_NCH = 2
FLASH_TQ = 512
FLASH_COLS = 256

W512 = ('a_r', 'a_k', 'a_v', 'a_z', 'b_q', 'b_k', 'b_v', 'b_z', 'c_q', 'c_k', 'c_v', 'c_z', 'x_q', 'x_z')
MERGE_W = N_BRANCH * D_MODEL
BLK512 = {name: MERGE_W // BR_W + i for i, name in enumerate(W512)}
LORA_BLK = (MERGE_W + len(W512) * BR_W) // LANES
CAB_BLK = LORA_BLK + 1
PROJ_W = (CAB_BLK + 1) * LANES
REF_OFF = {'a_r': 0, 'a_k': 512, 'a_v': 1024, 'a_lora': 1536, 'a_z': 1664, 'b_q': 2176, 'b_k': 2688, 'b_v': 3200,
           'b_z': 3712, 'c_q': 4224, 'c_k': 4736, 'c_v': 5248, 'c_ab': 5760, 'c_z': 5768, 'x_q': 6280, 'x_z': 6792,
           'merge': 7304}


def _cparams(*sem):
    return pltpu.CompilerParams(dimension_semantics=sem, vmem_limit_bytes=VMEM_LIMIT)


def _sigmoid(x):
    return 1.0 / (1.0 + jnp.exp(-x))


def _silu(x):
    return x * _sigmoid(x)


def _softplus(x):
    return jnp.maximum(x, 0.0) + jnp.log(1.0 + jnp.exp(-jnp.abs(x)))


def _dot(a, b, precision=None):
    return jnp.dot(a, b, preferred_element_type=F32, precision=precision)


def _dot_nt(a, b, precision=None):
    return lax.dot_general(a, b, (((1,), (1,)), ((), ())), preferred_element_type=F32, precision=precision)


def _iota(shape, axis):
    return lax.broadcasted_iota(jnp.int32, shape, axis)


def _split(x):
    hi = x.astype(BF16)
    return hi, (x - hi.astype(F32)).astype(BF16)


def _seg_allsum64(xs):
    same = (_iota((LANES, LANES), 0) // N_A) == (_iota((LANES, LANES), 1) // N_A)
    ones = jnp.where(same, 1.0, 0.0).astype(BF16)
    ones2 = jnp.concatenate([ones, ones], axis=0)
    return [_dot(jnp.concatenate(_split(x), axis=1), ones2) for x in xs]


def _lhs3(x):
    hi, lo = _split(x)
    return jnp.concatenate([hi, hi, lo], axis=1)


def _rhs3(x):
    hi, lo = _split(x)
    return jnp.concatenate([hi, lo, hi], axis=0)


def _bdot(a, b):
    return _dot(a.astype(BF16), b.astype(BF16))


def _bdot_nt(a, b):
    return _dot_nt(a.astype(BF16), b.astype(BF16))


def _tri_inverse(ls, split):
    n = ls[0].shape[0]
    eye = (_iota((n, n), 0) == _iota((n, n), 1)).astype(F32)
    lhs, rhs = (_lhs3, _rhs3) if split else ((lambda z: z.astype(BF16)),) * 2
    xs = [eye + l for l in ls]
    ps = [_dot(lhs(l), rhs(l)) for l in ls]
    for _ in range(4):
        xps = [_dot(lhs(jnp.concatenate([x, p], axis=0)), rhs(p)) for x, p in zip(xs, ps)]
        xs = [x + xp[:n] for x, xp in zip(xs, xps)]
        ps = [xp[n:] for xp in xps]
    return [x + _dot(lhs(x), rhs(p)) for x, p in zip(xs, ps)]


def _norm_matmul_kernel(x_ref, g_ref, w_ref, o_ref, h_ref):
    @pl.when(pl.program_id(1) == 0)
    def _():
        x = x_ref[...]
        y = x * lax.rsqrt(jnp.mean(x * x, axis=-1, keepdims=True) + NORM_EPS)
        h_ref[...] = (y * g_ref[...]).astype(BF16)
    o_ref[...] = _dot(h_ref[...], w_ref[...])


def _norm_matmul(x, g, w, tm, tn):
    n, d = x.shape
    c = w.shape[1]
    return pl.pallas_call(
        _norm_matmul_kernel,
        grid=(n // tm, c // tn),
        in_specs=[pl.BlockSpec((tm, d), lambda i, j: (i, 0)),
                  pl.BlockSpec((1, d), lambda i, j: (0, 0)),
                  pl.BlockSpec((d, tn), lambda i, j: (0, j))],
        out_specs=pl.BlockSpec((tm, tn), lambda i, j: (i, j)),
        out_shape=jax.ShapeDtypeStruct((n, c), F32),
        scratch_shapes=[pltpu.VMEM((tm, d), BF16)],
        compiler_params=_cparams("parallel", "arbitrary"),
    )(x, g.reshape(1, d), w)


def _final_norm_kernel(x_ref, g_ref, o_ref):
    x = x_ref[...]
    o_ref[...] = x * lax.rsqrt(jnp.mean(x * x, axis=-1, keepdims=True) + NORM_EPS) * g_ref[...]


def _final_norm(x, g, tm):
    n, d = x.shape
    return pl.pallas_call(
        _final_norm_kernel,
        grid=(n // tm,),
        in_specs=[pl.BlockSpec((tm, d), lambda i: (i, 0)), pl.BlockSpec((1, d), lambda i: (0, 0))],
        out_specs=pl.BlockSpec((tm, d), lambda i: (i, 0)),
        out_shape=jax.ShapeDtypeStruct((n, d), F32),
        compiler_params=_cparams("parallel"),
    )(x, g.reshape(1, d))


def _rwkv_prep_math(cur, prev, prm):
    mu_r, mu_k, mu_v, mu_l, w0, w2p, a0, a2p, kkp, kap = prm
    r = cur[0] + (prev[0] - cur[0]) * mu_r
    k = cur[1] + (prev[1] - cur[1]) * mu_k
    v = cur[2] + (prev[2] - cur[2]) * mu_v
    lo = cur[3] + (prev[3] - cur[3]) * mu_l
    w_log = -_softplus(-(w0 + _dot(jnp.tanh(lo), w2p))) - 0.5
    log_decay = -jnp.exp(w_log)
    a = _sigmoid(a0 + _dot(lo, a2p))
    kk = k * kkp
    kk2 = kk * kk
    ssq = _seg_allsum64([kk2[:, i * LANES:(i + 1) * LANES] for i in range(BR_W // LANES)])
    kk = kk * lax.rsqrt(jnp.concatenate(ssq, axis=1) + 1e-6)
    k = k * (1.0 + (a - 1.0) * kap)
    return r, log_decay, k, v, -kk, kk * a


def _rwkv_prep_sample_kernel(pr, pk, pv, plo, qr, qk, qv, qlo, mu_r, mu_k, mu_v, mu_l, w0, w2p, a0, a2p, kkp, kap,
                             o_r, o_w, o_k, o_v, o_a, o_b):
    cur = (pr[...], pk[...], pv[...], plo[...])
    prev = (qr[...], qk[...], qv[...], qlo[...])
    prm = tuple(z[...] for z in (mu_r, mu_k, mu_v, mu_l, w0, w2p, a0, a2p, kkp, kap))
    outs = _rwkv_prep_math(cur, prev, prm)
    for o, val in zip((o_r, o_w, o_k, o_v, o_a, o_b), outs):
        o[...] = val


def _full_spec(a):
    nd = a.ndim
    return pl.BlockSpec(a.shape, lambda *_: (0,) * nd)


def _rwkv_prep_sample(proj, shift, prm):
    n = proj.shape[0]

    def col(blk, w):
        return pl.BlockSpec((n, w), lambda i: (0, blk))
    out = pl.BlockSpec((n, BR_W), lambda i: (0, 0))
    return pl.pallas_call(
        _rwkv_prep_sample_kernel,
        grid=(1,),
        in_specs=[col(BLK512['a_r'], BR_W), col(BLK512['a_k'], BR_W), col(BLK512['a_v'], BR_W), col(LORA_BLK, LANES)]
                 + [_full_spec(s) for s in shift] + [_full_spec(p) for p in prm],
        out_specs=[out] * 6,
        out_shape=[jax.ShapeDtypeStruct((n, BR_W), F32)] * 6,
        compiler_params=_cparams("arbitrary"),
    )(proj, proj, proj, proj, *shift, *prm)


def _stack_pair(x):
    lane = _iota(x.shape, 1)
    return jnp.concatenate([jnp.where(lane < N_A, x, 0.0), jnp.where(lane >= N_A, x, 0.0)], axis=0)


def _rwkv_group_norm_bonus(ys, rs, ks, vs, gs, bs, rks):
    n = ys[0].shape[0]
    sums = _seg_allsum64([jnp.concatenate([y, r * k * rk], axis=0) for y, r, k, rk in zip(ys, rs, ks, rks)])
    ds = [y - s[:n] * (1.0 / N_A) for y, s in zip(ys, sums)]
    vars_ = _seg_allsum64([d * d for d in ds])
    return [d * lax.rsqrt(var * (1.0 / N_A) + RWKV_GN_EPS) * g + b + s[n:] * v
            for d, var, g, b, s, v in zip(ds, vars_, gs, bs, sums, vs)]


def _rwkv_chunk_kernel(pr, pk, pv, plo, mu_r, mu_k, mu_v, mu_l, w0, w2p, a0, a2p, kkp, kap, g_ref, bb_ref, rk_ref,
                       o_ref, s_ref, st_ref, c_r, c_k, c_v, c_l):
    ci = pl.program_id(1)
    nc = pl.num_programs(1)
    c = CHUNK
    carries = (c_r, c_k, c_v, c_l)

    @pl.when(ci == 0)
    def _():
        st_ref[...] = jnp.zeros_like(st_ref)
        for cr in carries:
            cr[...] = jnp.zeros_like(cr)
    cur = (pr[...], pk[...], pv[...], plo[...])
    rows = cur[0].shape[0]
    prev = [jnp.where(_iota(z.shape, 0) == 0, cr[...], pltpu.roll(z, 1, 0)) for z, cr in zip(cur, carries)]
    for z, cr in zip(cur, carries):
        cr[...] = z[rows - 1:rows, :]
    prm = tuple(z[...] for z in (mu_r, mu_k, mu_v, mu_l, w0, w2p, a0, a2p, kkp, kap))
    r, lw, k, v, a_in, b_in = _rwkv_prep_math(cur, prev, prm)
    ti = _iota((rows, rows), 0)
    tj = _iota((rows, rows), 1)
    tri = ((ti >= tj) & ((ti // c) == (tj // c))).astype(F32)
    cum = _dot(tri, lw, HI)
    p_incl = jnp.exp(cum)
    p_inv = jnp.exp(-cum)
    p_prev = jnp.exp(cum - lw)
    at = a_in * p_prev
    bt = b_in * p_inv
    kt = k * p_inv
    rt = r * p_incl
    n2 = 2 * c
    ri = _iota((n2, n2), 0)
    cj = _iota((n2, n2), 1)
    same = (ri >= c) == (cj >= c)
    strict = same & ((ri & (c - 1)) > (cj & (c - 1)))
    incl = same & ((ri & (c - 1)) >= (cj & (c - 1)))
    eye = (ri == cj).astype(F32)
    n_ch = rows // c
    n_pair = H_A // 2
    units = [(ch, p) for ch in range(n_ch) for p in range(n_pair)]

    def rsl(ch):
        return slice(ch * c, (ch + 1) * c)

    def lsl(p):
        return slice(p * LANES, (p + 1) * LANES)
    stk = {name: [_stack_pair(z[rsl(ch), lsl(p)]) for ch, p in units]
           for name, z in (('a', at), ('b', bt), ('k', kt), ('r', rt), ('v', v))}
    prods = [_dot_nt(jnp.concatenate([a_s, r_s], axis=0).astype(BF16),
                     jnp.concatenate([b_s, k_s], axis=0).astype(BF16))
             for a_s, r_s, b_s, k_s in zip(stk['a'], stk['r'], stk['b'], stk['k'])]
    l_ab = [jnp.where(strict, pr[:n2, :n2], 0.0) for pr in prods]
    l_ak = [jnp.where(strict, pr[:n2, n2:], 0.0).astype(BF16) for pr in prods]
    m_r = [jnp.concatenate([jnp.where(incl, pr[n2:, :n2], 0.0), jnp.where(incl, pr[n2:, n2:], 0.0)],
                           axis=1).astype(BF16) for pr in prods]
    v_b = [z.astype(BF16) for z in stk['v']]
    lv = [_dot(l, z) for l, z in zip(l_ak, v_b)]
    t_inv = _tri_inverse(l_ab, split=False)
    wu = [_bdot(t, jnp.concatenate([a_s, z], axis=1)) for t, a_s, z in zip(t_inv, stk['a'], lv)]
    p_last = [p_incl[(ch + 1) * c - 1:(ch + 1) * c, lsl(p)] for ch, p in units]
    bk_t = [jnp.concatenate([(b_s * pl_).T, (k_s * pl_).T], axis=1).astype(BF16)
            for b_s, k_s, pl_ in zip(stk['b'], stk['k'], p_last)]
    p_col = [jnp.sum(eye * pl_, axis=1, keepdims=True) for pl_ in p_last]
    r_b = [z.astype(BF16) for z in stk['r']]
    s_t = [st_ref[p] for p in range(n_pair)]
    for ch in range(n_ch):
        ix = [ch * n_pair + p for p in range(n_pair)]
        s_b = [s.astype(BF16) for s in s_t]
        u_b = [(_dot(wu[i][:, :LANES].astype(BF16), s) + wu[i][:, LANES:]).astype(BF16) for i, s in zip(ix, s_b)]
        y_s = [_dot(jnp.concatenate([r_b[i], m_r[i]], axis=1), jnp.concatenate([s, u, v_b[i]], axis=0))
               for i, s, u in zip(ix, s_b, u_b)]
        s_t = [s * p_col[i] + _dot(bk_t[i], jnp.concatenate([u, v_b[i]], axis=0)) for i, s, u in zip(ix, s_t, u_b)]
        rs = rsl(ch)
        pairs = [lsl(p) for p in range(n_pair)]
        outs = _rwkv_group_norm_bonus([z[:c] + z[c:] for z in y_s], [r[rs, sl] for sl in pairs],
                                      [k[rs, sl] for sl in pairs], [v[rs, sl] for sl in pairs],
                                      [g_ref[:, sl] for sl in pairs], [bb_ref[:, sl] for sl in pairs],
                                      [rk_ref[:, sl] for sl in pairs])
        for sl, out in zip(pairs, outs):
            o_ref[rs, sl] = out
    for p in range(n_pair):
        st_ref[p] = s_t[p]

    @pl.when(ci == nc - 1)
    def _():
        for p in range(H_A // 2):
            s_vk = st_ref[p].T
            s_ref[2 * p] = s_vk[:N_A, :N_A]
            s_ref[2 * p + 1] = s_vk[N_A:, N_A:]


def _rwkv_chunk(proj, prm, ln_g, ln_b, rk):
    bsz, t, _ = proj.shape
    rows = min(RWKV_NCH * CHUNK, t)

    def col(blk, w):
        return pl.BlockSpec((None, rows, w), lambda bi, ci: (bi, ci, blk))
    par = pl.BlockSpec((1, BR_W), lambda bi, ci: (0, 0))
    return pl.pallas_call(
        _rwkv_chunk_kernel,
        grid=(bsz, t // rows),
        in_specs=[col(BLK512['a_r'], BR_W), col(BLK512['a_k'], BR_W), col(BLK512['a_v'], BR_W), col(LORA_BLK, LANES)]
                 + [_full_spec(z) for z in prm] + [par] * 3,
        out_specs=[pl.BlockSpec((None, rows, BR_W), lambda bi, ci: (bi, ci, 0)),
                   pl.BlockSpec((None, H_A, N_A, N_A), lambda bi, ci: (bi, 0, 0, 0))],
        out_shape=[jax.ShapeDtypeStruct((bsz, t, BR_W), F32), jax.ShapeDtypeStruct((bsz, H_A, N_A, N_A), F32)],
        scratch_shapes=[pltpu.VMEM((H_A // 2, LANES, LANES), F32)] + [pltpu.VMEM((1, BR_W), F32)] * 3
                       + [pltpu.VMEM((1, LANES), F32)],
        compiler_params=_cparams("parallel", "arbitrary"),
    )(proj, proj, proj, proj, *prm, ln_g, ln_b, rk)


def _rwkv_step_kernel(r_ref, w_ref, k_ref, v_ref, a_ref, b_ref, g_ref, bb_ref, rk_ref, s_ref, o_ref, so_ref):
    nb = s_ref.shape[0]
    eye = (_iota((N_A, N_A), 0) == _iota((N_A, N_A), 1)).astype(F32)

    def body(bi, carry):
        for h in range(H_A):
            s = s_ref[bi, h]
            r, k, v = r_ref[bi, h], k_ref[bi, h], v_ref[bi, h]
            sa = jnp.sum(s * a_ref[bi, h], axis=1, keepdims=True)
            v_col = jnp.sum(eye * v, axis=1, keepdims=True)
            s2 = s * jnp.exp(w_ref[bi, h]) + sa * b_ref[bi, h] + v_col * k
            y_col = jnp.sum(s2 * r, axis=1, keepdims=True)
            y = jnp.sum(eye * y_col, axis=0, keepdims=True)
            mu = jnp.mean(y, axis=1, keepdims=True)
            d = y - mu
            var = jnp.mean(d * d, axis=1, keepdims=True)
            yn = d * lax.rsqrt(var + RWKV_GN_EPS) * g_ref[h] + bb_ref[h]
            o_ref[bi, h] = yn + jnp.sum(r * k * rk_ref[h], axis=1, keepdims=True) * v
            so_ref[bi, h] = s2
        return carry
    lax.fori_loop(0, nb, body, 0)


def _rwkv_step(rows, ln_g, ln_b, rk, state, layer, nb):
    n = rows[0].shape[0]
    rows = [z.reshape(n, H_A, 1, N_A) for z in rows]
    prm = [z.reshape(H_A, 1, N_A) for z in (ln_g, ln_b, rk)]
    row = pl.BlockSpec((nb, H_A, 1, N_A), lambda i: (i, 0, 0, 0))
    par = pl.BlockSpec((H_A, 1, N_A), lambda i: (0, 0, 0))
    st_in = pl.BlockSpec((None, nb, H_A, N_A, N_A), lambda i: (layer, i, 0, 0, 0))
    st_out = pl.BlockSpec((nb, H_A, N_A, N_A), lambda i: (i, 0, 0, 0))
    o, s_new = pl.pallas_call(
        _rwkv_step_kernel,
        grid=(n // nb,),
        in_specs=[row] * 6 + [par] * 3 + [st_in],
        out_specs=[row, st_out],
        out_shape=[jax.ShapeDtypeStruct((n, H_A, 1, N_A), F32), jax.ShapeDtypeStruct((n, H_A, N_A, N_A), F32)],
        compiler_params=_cparams("parallel"),
    )(*rows, *prm, state)
    return o.reshape(n, BR_W), s_new


def _gdn_norm_qk(q, k):
    qs, ks = [], []
    for h in range(H_C):
        sl = slice(h * DK_C, (h + 1) * DK_C)
        qh, kh = q[:, sl], k[:, sl]
        qs.append(qh * lax.rsqrt(jnp.sum(qh * qh, axis=1, keepdims=True) + 1e-6) * (DK_C ** -0.5))
        ks.append(kh * lax.rsqrt(jnp.sum(kh * kh, axis=1, keepdims=True) + 1e-6))
    return jnp.concatenate(qs, axis=1), jnp.concatenate(ks, axis=1)


def _gdn_gate(ab, alog, dtb):
    lane = _iota(ab.shape, 1)
    return jnp.where(lane < H_C, -jnp.exp(alog) * _softplus(ab + dtb), _sigmoid(ab))


def _gdn_causal_conv(x, w_ref, tail_ref):
    tm = x.shape[0]
    tail = tail_ref[...]
    acc = x * w_ref[CONV_W - 1:CONV_W, :]
    row8 = _iota(tail.shape, 0)
    for s in range(1, CONV_W):
        rolled = pltpu.roll(x, s, 0)
        head = jnp.where(row8 < s, pltpu.roll(tail, s, 0), rolled[:SUBLANES])
        xs = jnp.concatenate([head, rolled[SUBLANES:]], axis=0)
        acc = acc + xs * w_ref[CONV_W - 1 - s:CONV_W - s, :]
    tail_ref[...] = x[tm - SUBLANES:, :]
    return _silu(acc)


def _gdn_prep_sample_kernel(xq, xk, xv, xab, bq, bk, bv, wq, wk, wv, alog, dtb, o_q, o_k, o_v, o_eg, o_beta):
    outs = []
    for x_ref, b_ref, w_ref in zip((xq, xk, xv), (bq, bk, bv), (wq, wk, wv)):
        acc = x_ref[...] * w_ref[CONV_W - 1:CONV_W, :]
        for i in range(CONV_W - 1):
            acc = acc + b_ref[i] * w_ref[i:i + 1, :]
        outs.append(_silu(acc))
    q, k = _gdn_norm_qk(outs[0], outs[1])
    o_q[...] = q
    o_k[...] = k
    o_v[...] = outs[2]
    gb = _gdn_gate(xab[...], alog[...], dtb[...])
    n = gb.shape[0]
    eg = jnp.exp(gb)
    o_eg[...] = jnp.concatenate([jnp.broadcast_to(eg[:, h:h + 1], (n, DV_C)) for h in range(H_C)], axis=1)
    o_beta[...] = jnp.concatenate([jnp.broadcast_to(gb[:, H_C + h:H_C + h + 1], (n, DV_C)) for h in range(H_C)],
                                  axis=1)


def _gdn_prep_sample(proj, bufs, prm):
    n = proj.shape[0]

    def col(blk, w):
        return pl.BlockSpec((n, w), lambda i: (0, blk))
    out = pl.BlockSpec((n, BR_W), lambda i: (0, 0))
    return pl.pallas_call(
        _gdn_prep_sample_kernel,
        grid=(1,),
        in_specs=[col(BLK512['c_q'], BR_W), col(BLK512['c_k'], BR_W), col(BLK512['c_v'], BR_W), col(CAB_BLK, LANES)]
                 + [_full_spec(z) for z in bufs] + [_full_spec(p) for p in prm],
        out_specs=[out] * 5,
        out_shape=[jax.ShapeDtypeStruct((n, BR_W), F32)] * 5,
        compiler_params=_cparams("arbitrary"),
    )(proj, proj, proj, proj, *bufs, *prm)


def _gdn_out_norm(o, g, z):
    return o * lax.rsqrt(jnp.mean(o * o, axis=1, keepdims=True) + NORM_EPS) * g * _silu(z)


def _gdn_chunk_kernel(xq, xk, xv, xab, z_ref, wq, wk, wv, alog, dtb, g_ref, o_ref, s_ref, st_ref, t_q, t_k, t_v):
    ci = pl.program_id(1)
    nc = pl.num_programs(1)
    c = CHUNK
    rows = xab.shape[0]
    n_ch = rows // c

    @pl.when(ci == 0)
    def _():
        st_ref[...] = jnp.zeros_like(st_ref)
        for tl in (t_q, t_k, t_v):
            tl[...] = jnp.zeros_like(tl)
    q_all, k_all = _gdn_norm_qk(_gdn_causal_conv(xq[...], wq, t_q), _gdn_causal_conv(xk[...], wk, t_k))
    v_all = _gdn_causal_conv(xv[...], wv, t_v)
    gb = _gdn_gate(xab[...], alog[...], dtb[...])
    ri = _iota((rows, rows), 0)
    cj = _iota((rows, rows), 1)
    same = (ri // c) == (cj // c)
    incl = same & (ri >= cj)
    strict = same & (ri > cj)
    cum = _dot(incl.astype(F32), gb, HI)
    cum_t = cum.T
    heads = range(H_C)
    hsl = [slice(h * DK_C, (h + 1) * DK_C) for h in heads]
    gc_col = [cum[:, h:h + 1] for h in heads]
    beta = [gb[:, H_C + h:H_C + h + 1] for h in heads]
    decay = [jnp.where(incl, jnp.exp(jnp.where(incl, gc_col[h] - cum_t[h:h + 1, :], 0.0)), 0.0) for h in heads]
    q = [q_all[:, sl] for sl in hsl]
    k = [k_all[:, sl] for sl in hsl]
    kb = [k[h] * beta[h] for h in heads]
    prod = [_dot_nt(jnp.concatenate([kb[h], q[h]], axis=0).astype(BF16), k[h].astype(BF16)) for h in heads]
    t_inv = _tri_inverse([-jnp.where(strict, prod[h][:rows] * decay[h], 0.0) for h in heads], split=True)
    qk = [jnp.where(incl, prod[h][rows:] * decay[h], 0.0) for h in heads]
    uw = [_bdot(t_inv[h], jnp.concatenate([v_all[:, hsl[h]] * beta[h], kb[h] * jnp.exp(gc_col[h])], axis=1))
          for h in heads]
    qg = [q[h] * jnp.exp(gc_col[h]) for h in heads]
    s = [st_ref[h] for h in heads]
    for ch in range(n_ch):
        rs = slice(ch * c, (ch + 1) * c)
        g_last = [cum[(ch + 1) * c - 1:(ch + 1) * c, h:h + 1] for h in heads]
        s_b = [z.astype(BF16) for z in s]
        v_nb = [(uw[h][rs, :DV_C] - _dot(uw[h][rs, DV_C:].astype(BF16), s_b[h])).astype(BF16) for h in heads]
        zero = jnp.zeros_like(v_nb[0])
        o = [_dot(jnp.concatenate([qg[h][rs], qk[h][rs]], axis=1).astype(BF16),
                  jnp.concatenate([s_b[h]] + [v_nb[h] if i == ch else zero for i in range(n_ch)], axis=0))
             for h in heads]
        s = [s[h] * jnp.exp(g_last[h]) + _dot((k[h][rs] * jnp.exp(g_last[h] - gc_col[h][rs])).T.astype(BF16), v_nb[h])
             for h in heads]
        for h in heads:
            o_ref[rs, hsl[h]] = _gdn_out_norm(o[h], g_ref[...], z_ref[rs, hsl[h]])
    for h in heads:
        st_ref[h] = s[h]

    @pl.when(ci == nc - 1)
    def _():
        s_ref[...] = st_ref[...]


def _gdn_chunk(proj, prm, norm_g):
    bsz, t, _ = proj.shape
    rows = min(GDN_NCH * CHUNK, t)

    def col(blk, w):
        return pl.BlockSpec((None, rows, w), lambda bi, ci: (bi, ci, blk))
    return pl.pallas_call(
        _gdn_chunk_kernel,
        grid=(bsz, t // rows),
        in_specs=[col(BLK512['c_q'], BR_W), col(BLK512['c_k'], BR_W), col(BLK512['c_v'], BR_W), col(CAB_BLK, LANES),
                  col(BLK512['c_z'], BR_W)] + [_full_spec(z) for z in prm]
                 + [pl.BlockSpec((1, DV_C), lambda bi, ci: (0, 0))],
        out_specs=[pl.BlockSpec((None, rows, BR_W), lambda bi, ci: (bi, ci, 0)),
                   pl.BlockSpec((None, H_C, DK_C, DV_C), lambda bi, ci: (bi, 0, 0, 0))],
        out_shape=[jax.ShapeDtypeStruct((bsz, t, BR_W), F32), jax.ShapeDtypeStruct((bsz, H_C, DK_C, DV_C), F32)],
        scratch_shapes=[pltpu.VMEM((H_C, DK_C, DV_C), F32)] + [pltpu.VMEM((SUBLANES, BR_W), F32)] * 3,
        compiler_params=_cparams("parallel", "arbitrary"),
    )(proj, proj, proj, proj, proj, *prm, norm_g.reshape(1, DV_C))


def _gdn_step_kernel(q_ref, k_ref, v_ref, eg_ref, beta_ref, z_ref, g_ref, s_ref, o_ref, so_ref):
    nb = s_ref.shape[0]
    eye = (_iota((DK_C, DK_C), 0) == _iota((DK_C, DK_C), 1)).astype(F32)

    def body(bi, carry):
        for h in range(H_C):
            s = s_ref[bi, h]
            eg = eg_ref[bi, h]
            k_col = jnp.sum(eye * k_ref[bi, h], axis=1, keepdims=True)
            q_col = jnp.sum(eye * q_ref[bi, h], axis=1, keepdims=True)
            ks = jnp.sum(s * k_col, axis=0, keepdims=True)
            u = beta_ref[bi, h] * (v_ref[bi, h] - eg * ks)
            s2 = s * eg + k_col * u
            o = jnp.sum(s2 * q_col, axis=0, keepdims=True)
            o_ref[bi, h] = _gdn_out_norm(o, g_ref[...], z_ref[bi, h])
            so_ref[bi, h] = s2
        return carry
    lax.fori_loop(0, nb, body, 0)


def _gdn_step(q, k, v, eg, beta, z, norm_g, state, layer, nb):
    n = q.shape[0]
    rows = [x.reshape(n, H_C, 1, DV_C) for x in (q, k, v, eg, beta, z)]
    row = pl.BlockSpec((nb, H_C, 1, DV_C), lambda i: (i, 0, 0, 0))
    st_in = pl.BlockSpec((None, nb, H_C, DK_C, DV_C), lambda i: (layer, i, 0, 0, 0))
    st_out = pl.BlockSpec((nb, H_C, DK_C, DV_C), lambda i: (i, 0, 0, 0))
    o, s_new = pl.pallas_call(
        _gdn_step_kernel,
        grid=(n // nb,),
        in_specs=[row] * 6 + [pl.BlockSpec((1, DV_C), lambda i: (0, 0)), st_in],
        out_specs=[row, st_out],
        out_shape=[jax.ShapeDtypeStruct((n, H_C, 1, DV_C), F32), jax.ShapeDtypeStruct((n, H_C, DK_C, DV_C), F32)],
        compiler_params=_cparams("parallel"),
    )(*rows, norm_g.reshape(1, DV_C), state)
    return o.reshape(n, BR_W), s_new


def _diff_lambda(lam_ref, lam_init):
    lv = lam_ref[...]
    s1 = jnp.sum(lv[0:1] * lv[1:2], axis=1, keepdims=True)
    s2 = jnp.sum(lv[2:3] * lv[3:4], axis=1, keepdims=True)
    return jnp.exp(s1) - jnp.exp(s2) + lam_init


def _diff_out_norm(o, g, lam_init):
    return o * lax.rsqrt(jnp.mean(o * o, axis=1, keepdims=True) + 1e-5) * g * (1.0 - lam_init)


def _diff_flash_kernel(q_ref, k_ref, v_ref, lam_ref, g_ref, o_ref, qs_ref, m_ref, l_ref, acc_ref, s_ref, *, lam_init):
    i = pl.program_id(2)
    tq = q_ref.shape[0]
    q = q_ref[...] * (DK_B ** -0.5)
    lane = _iota(q.shape, 1)
    qs_ref[:tq] = jnp.where(lane < DK_B, q, 0.0).astype(BF16)
    qs_ref[tq:] = jnp.where(lane >= DK_B, q, 0.0).astype(BF16)
    m_ref[...] = jnp.full_like(m_ref, NEG_INF)
    l_ref[...] = jnp.zeros_like(l_ref)
    acc_ref[...] = jnp.zeros_like(acc_ref)

    n_grp = 2 * tq // FLASH_COLS
    grp = [slice(g * FLASH_COLS, (g + 1) * FLASH_COLS) for g in range(n_grp)]

    def kv_rows(j):
        return pl.ds(pl.multiple_of(j * tq, tq), tq)

    def scores(j):
        s_ref[j % 2] = _dot_nt(k_ref[kv_rows(j), :].astype(BF16), qs_ref[...])

    def consume(j, diagonal):
        v_t = v_ref[kv_rows(j), :].T.astype(BF16)
        slot = j % 2
        for g, sl in enumerate(grp):
            s = s_ref[slot, :, sl]
            if diagonal:
                s = jnp.where(_iota(s.shape, 0) <= ((_iota(s.shape, 1) + g * FLASH_COLS) & (tq - 1)), s, NEG_INF)
            m_old = m_ref[:, sl]
            m_new = jnp.maximum(m_old, jnp.max(s, axis=0, keepdims=True))
            alpha = jnp.exp(m_old - m_new)
            p = jnp.exp(s - m_new)
            l_ref[:, sl] = alpha * l_ref[:, sl] + jnp.sum(p, axis=0, keepdims=True)
            acc_ref[:, sl] = alpha * acc_ref[:, sl] + _dot(v_t, p.astype(BF16))
            m_ref[:, sl] = m_new

    scores(0)

    def body(j, carry):
        scores(j)
        consume(j - 1, False)
        return carry
    lax.fori_loop(1, i + 1, body, 0)
    consume(i, True)
    lam = _diff_lambda(lam_ref, lam_init)
    on = acc_ref[...] / l_ref[...]
    o = (on[:, :tq] - lam * on[:, tq:]).T
    o_ref[...] = _diff_out_norm(o, g_ref[...], lam_init)


def _diff_flash(proj, lam, norm_g, lam_init, tq):
    bsz, t, _ = proj.shape
    kq, kk, kv = (BLK512[n] * (BR_W // LANES) for n in ('b_q', 'b_k', 'b_v'))
    return pl.pallas_call(
        functools.partial(_diff_flash_kernel, lam_init=lam_init),
        grid=(bsz, H_B, t // tq),
        in_specs=[pl.BlockSpec((None, tq, LANES), lambda b, h, i: (b, i, kq + h)),
                  pl.BlockSpec((None, t, LANES), lambda b, h, i: (b, 0, kk + h)),
                  pl.BlockSpec((None, t, LANES), lambda b, h, i: (b, 0, kv + h)),
                  pl.BlockSpec((4, DK_B), lambda b, h, i: (0, 0)),
                  pl.BlockSpec((1, DV_B), lambda b, h, i: (0, 0))],
        out_specs=pl.BlockSpec((None, tq, LANES), lambda b, h, i: (b, i, h)),
        out_shape=jax.ShapeDtypeStruct((bsz, t, BR_W), F32),
        scratch_shapes=[pltpu.VMEM((2 * tq, LANES), BF16), pltpu.VMEM((1, 2 * tq), F32),
                        pltpu.VMEM((1, 2 * tq), F32), pltpu.VMEM((DV_B, 2 * tq), F32),
                        pltpu.VMEM((2, tq, 2 * tq), F32)],
        compiler_params=_cparams("parallel", "parallel", "arbitrary"),
    )(proj, proj, proj, lam, norm_g.reshape(1, DV_B))


def _head_rows(x, n_rep):
    rows = []
    for h in range(x.shape[1] // LANES):
        rows += [x[:, h * LANES:(h + 1) * LANES]] * n_rep
    return jnp.concatenate(rows, axis=0)


def _diff_decode_kernel(pt_ref, q_ref, kn_ref, vn_ref, *rest, lam_init, npg):
    del pt_ref
    k_refs, v_refs = rest[:npg], rest[npg:2 * npg]
    lam_ref, g_ref, o_ref = rest[2 * npg:]
    nrow = 2 * H_B
    q8 = _head_rows(q_ref[...] * (DK_B ** -0.5), 2)
    lane = _iota(q8.shape, 1)
    row = _iota(q8.shape, 0)
    q8 = jnp.where((lane >= DK_B) == ((row & 1) == 1), q8, 0.0)
    q8b = q8.astype(BF16)
    ncol = k_refs[0].shape[0]
    valid = (_iota((nrow, ncol), 1) & (H_B - 1)) == (_iota((nrow, ncol), 0) >> 1)
    s_new = jnp.sum(q8 * _head_rows(kn_ref[...], 2), axis=1, keepdims=True)
    s_pages = [jnp.where(valid, _dot_nt(q8b, k[...].astype(BF16)), NEG_INF) for k in k_refs]
    m = s_new
    for s in s_pages:
        m = jnp.maximum(m, jnp.max(s, axis=1, keepdims=True))
    p_new = jnp.exp(s_new - m)
    l_sum = p_new
    acc = p_new * _head_rows(vn_ref[...], 2)
    for s, v in zip(s_pages, v_refs):
        p = jnp.exp(s - m)
        l_sum = l_sum + jnp.sum(p, axis=1, keepdims=True)
        acc = acc + _dot(p.astype(BF16), v[...].astype(BF16))
    on = acc / l_sum
    lam = _diff_lambda(lam_ref, lam_init)
    for h in range(H_B):
        o = on[2 * h:2 * h + 1] - lam * on[2 * h + 1:2 * h + 2]
        o_ref[:, h * DV_B:(h + 1) * DV_B] = _diff_out_norm(o, g_ref[...], lam_init)


def _diff_decode(proj, cache_k, cache_v, page_table, lam, norm_g, lam_init, layer):
    n = proj.shape[0]
    npg = page_table.shape[1]

    def col(name):
        return pl.BlockSpec((None, 1, BR_W), lambda b, pt: (b, 0, BLK512[name]))

    def page(i):
        return pl.BlockSpec((None, None, PAGE * H_B, DV_B), lambda b, pt: (layer, pt[b * npg + i], 0, 0))
    pages = [page(i) for i in range(npg)]
    grid_spec = pltpu.PrefetchScalarGridSpec(
        num_scalar_prefetch=1,
        grid=(n,),
        in_specs=[col('b_q'), col('b_k'), col('b_v')] + pages + pages
                 + [pl.BlockSpec((4, DK_B), lambda b, pt: (0, 0)), pl.BlockSpec((1, DV_B), lambda b, pt: (0, 0))],
        out_specs=pl.BlockSpec((None, 1, BR_W), lambda b, pt: (b, 0, 0)))
    return pl.pallas_call(
        functools.partial(_diff_decode_kernel, lam_init=lam_init, npg=npg),
        grid_spec=grid_spec,
        out_shape=jax.ShapeDtypeStruct((n, 1, BR_W), F32),
        compiler_params=_cparams("parallel"),
    )(page_table.reshape(-1), proj, proj, proj, *([cache_k] * npg), *([cache_v] * npg), lam,
      norm_g.reshape(1, DV_B))


def _cross_kernel(q_ref, k_ref, v_ref, o_ref):
    hsl = [slice(h * HD_X, (h + 1) * HD_X) for h in range(H_X)]
    s = [_dot_nt(q_ref[:, sl].astype(BF16), k_ref[:, sl].astype(BF16)) * (HD_X ** -0.5) for sl in hsl]
    p = [jnp.exp(z - jnp.max(z, axis=1, keepdims=True)) for z in s]
    o = [_dot(z.astype(BF16), v_ref[:, sl].astype(BF16)) for z, sl in zip(p, hsl)]
    for sl, z, oz in zip(hsl, p, o):
        o_ref[:, sl] = oz / jnp.sum(z, axis=1, keepdims=True)


def _cross_prompt(proj, mem_kv, tq):
    bsz, t, _ = proj.shape
    n_mem = mem_kv.shape[1]
    return pl.pallas_call(
        _cross_kernel,
        grid=(bsz, t // tq),
        in_specs=[pl.BlockSpec((None, tq, BR_W), lambda b, i: (b, i, BLK512['x_q'])),
                  pl.BlockSpec((None, n_mem, BR_W), lambda b, i: (b, 0, 0)),
                  pl.BlockSpec((None, n_mem, BR_W), lambda b, i: (b, 0, 1))],
        out_specs=pl.BlockSpec((None, tq, BR_W), lambda b, i: (b, i, 0)),
        out_shape=jax.ShapeDtypeStruct((bsz, t, BR_W), F32),
        compiler_params=_cparams("parallel", "parallel"),
    )(proj, mem_kv, mem_kv)


def _cross_decode_kernel(q_ref, k_ref, v_ref, o_ref):
    nb = q_ref.shape[0]
    ncol = k_ref.shape[1]
    valid = (_iota((SUBLANES, ncol), 1) & (H_X - 1)) == _iota((SUBLANES, ncol), 0)
    for b in range(nb):
        q8 = jnp.concatenate([_head_rows(q_ref[b], 1), jnp.zeros((SUBLANES - H_X, HD_X), F32)], axis=0)
        s = _dot_nt(q8.astype(BF16), k_ref[b].astype(BF16)) * (HD_X ** -0.5)
        s = jnp.where(valid, s, NEG_INF)
        m = jnp.max(s, axis=1, keepdims=True)
        p = jnp.exp(s - m)
        o = _dot(p.astype(BF16), v_ref[b].astype(BF16)) / jnp.sum(p, axis=1, keepdims=True)
        for h in range(H_X):
            o_ref[b, :, h * HD_X:(h + 1) * HD_X] = o[h:h + 1]


def _cross_decode(proj, cache_k, cache_v, layer, nb):
    n = proj.shape[0]
    rows = cache_k.shape[2]
    kv = pl.BlockSpec((None, nb, rows, HD_X), lambda i: (layer, i, 0, 0))
    return pl.pallas_call(
        _cross_decode_kernel,
        grid=(n // nb,),
        in_specs=[pl.BlockSpec((nb, 1, BR_W), lambda i: (i, 0, BLK512['x_q'])), kv, kv],
        out_specs=pl.BlockSpec((nb, 1, BR_W), lambda i: (i, 0, 0)),
        out_shape=jax.ShapeDtypeStruct((n, 1, BR_W), F32),
        compiler_params=_cparams("parallel"),
    )(proj, cache_k, cache_v)


def _merge_kernel(x_ref, oa_ref, az_ref, ob_ref, bz_ref, oc_ref, ox_ref, xz_ref, mg_ref, wb_ref, wo_ref, o_ref):
    branches = (oa_ref[...] * _silu(az_ref[...]), ob_ref[...] * _silu(bz_ref[...]), oc_ref[...],
                ox_ref[...] * _silu(xz_ref[...]))
    acc = None
    for n, br in enumerate(branches):
        gate = _sigmoid(mg_ref[:, n * D_MODEL:(n + 1) * D_MODEL])
        term = gate * _dot(br.astype(BF16), wb_ref[n])
        acc = term if acc is None else acc + term
    o_ref[...] = x_ref[...] + _dot(acc.astype(BF16), wo_ref[...])


def _merge(x, proj, o_a, o_b, o_c, o_x, w_branch, w_out, tm):
    n = x.shape[0]

    def col(name):
        return pl.BlockSpec((tm, BR_W), lambda i: (i, BLK512[name]))
    br = pl.BlockSpec((tm, BR_W), lambda i: (i, 0))
    return pl.pallas_call(
        _merge_kernel,
        grid=(n // tm,),
        in_specs=[pl.BlockSpec((tm, D_MODEL), lambda i: (i, 0)),
                  br, col('a_z'), br, col('b_z'), br, br, col('x_z'),
                  pl.BlockSpec((tm, MERGE_W), lambda i: (i, 0)),
                  pl.BlockSpec((N_BRANCH, BR_W, D_MODEL), lambda i: (0, 0, 0)),
                  pl.BlockSpec((D_MODEL, D_MODEL), lambda i: (0, 0))],
        out_specs=pl.BlockSpec((tm, D_MODEL), lambda i: (i, 0)),
        out_shape=jax.ShapeDtypeStruct((n, D_MODEL), F32),
        compiler_params=_cparams("parallel"),
    )(x, o_a, proj, o_b, proj, o_c, o_x, proj, proj, w_branch, w_out)


def _kv_layout_kernel(*refs):
    k_ref, v_ref, ok_ref, ov_ref = refs[0], refs[1], refs[-2], refs[-1]
    tm = k_ref.shape[0]
    for src, dst in ((k_ref, ok_ref), (v_ref, ov_ref)):
        for h in range(H_B):
            dst[pl.ds(h, tm, stride=H_B), :] = src[:, h * LANES:(h + 1) * LANES]


def _kv_layout(proj, slot, depth, prev, tm):
    bsz, t, _ = proj.shape
    shape = jax.ShapeDtypeStruct((depth, bsz, t * H_B, LANES), F32)
    out = pl.BlockSpec((None, None, tm * H_B, LANES), lambda b, i: (slot, b, i, 0))
    in_specs = [pl.BlockSpec((None, tm, BR_W), lambda b, i: (b, i, BLK512['b_k'])),
                pl.BlockSpec((None, tm, BR_W), lambda b, i: (b, i, BLK512['b_v']))]
    args = [proj, proj]
    aliases = {}
    if prev is not None:
        in_specs += [pl.BlockSpec(memory_space=pl.ANY)] * 2
        args += list(prev)
        aliases = {2: 0, 3: 1}
    return pl.pallas_call(
        _kv_layout_kernel,
        grid=(bsz, t // tm),
        in_specs=in_specs,
        out_specs=[out, out],
        out_shape=[shape, shape],
        input_output_aliases=aliases,
        compiler_params=_cparams("parallel", "parallel"),
    )(*args)


def _merge_token_head(cache):
    return cache.reshape(*cache.shape[:-3], cache.shape[-3] * cache.shape[-2], cache.shape[-1])


def _permute_w_in(w_in):
    d = w_in.shape[0]
    parts = [w_in[..., REF_OFF['merge']:REF_OFF['merge'] + MERGE_W]]
    parts += [w_in[..., REF_OFF[n]:REF_OFF[n] + BR_W] for n in W512]
    parts.append(w_in[..., REF_OFF['a_lora']:REF_OFF['a_lora'] + 2 * LORA])
    parts.append(w_in[..., REF_OFF['c_ab']:REF_OFF['c_ab'] + 2 * H_C])
    parts.append(jnp.zeros(w_in.shape[:-1] + (LANES - 2 * H_C,), w_in.dtype))
    del d
    return jnp.concatenate(parts, axis=-1).astype(BF16)


def _layer_params(p, l):
    mu = p['rwkv_mu'][l]
    row = lambda z: z.reshape(1, -1)
    zeros64 = jnp.zeros((LORA, BR_W), F32)
    rwkv = (row(mu[:BR_W]), row(mu[BR_W:2 * BR_W]), row(mu[2 * BR_W:3 * BR_W]), row(mu[3 * BR_W:]),
            row(p['rwkv_w0'][l]), jnp.concatenate([p['rwkv_w2'][l], zeros64], axis=0),
            row(p['rwkv_a0'][l]), jnp.concatenate([zeros64, p['rwkv_a2'][l]], axis=0),
            row(p['rwkv_kk'][l]), row(p['rwkv_ka'][l]))
    wc = p['gdn_conv_w'][l]
    pad = lambda z: jnp.pad(z, (0, LANES - z.shape[0])).reshape(1, LANES)
    gdn = (wc[:, :BR_W], wc[:, BR_W:2 * BR_W], wc[:, 2 * BR_W:], pad(p['gdn_a_log'][l]), pad(p['gdn_dt_bias'][l]))
    return rwkv, gdn


def _prompt_layer(x, mem, p, wts, l, kv=None, slot=0, depth=1):
    bsz, t, d = x.shape
    n = bsz * t
    rwkv_prm, gdn_prm = _layer_params(p, l)
    lam_init = 0.8 - 0.6 * math.exp(-0.3 * l)
    mem_kv = _norm_matmul(mem.reshape(-1, d), p['norm_mem_g'][l], wts['w_mem'][l], 256, 512)
    mem_kv = mem_kv.reshape(bsz, -1, 2 * BR_W)
    proj = _norm_matmul(x.reshape(n, d), p['norm_g'][l], wts['w_in'][l], min(PROJ_TM, n), PROJ_TN)
    proj3 = proj.reshape(bsz, t, PROJ_W)
    row = lambda z: z.reshape(1, -1)
    o_a, s_rwkv = _rwkv_chunk(proj3, rwkv_prm, row(p['rwkv_ln_g'][l]), row(p['rwkv_ln_b'][l]), row(p['rwkv_rk'][l]))
    o_b = _diff_flash(proj3, p['diff_lam'][l], p['diff_norm_g'][l], lam_init, min(FLASH_TQ, t // 2))
    o_c, s_gdn = _gdn_chunk(proj3, gdn_prm, p['gdn_norm_g'][l])
    o_x = _cross_prompt(proj3, mem_kv, min(512, t))
    flat = lambda z: z.reshape(n, BR_W)
    x_new = _merge(x.reshape(n, d), proj, flat(o_a), flat(o_b), flat(o_c), flat(o_x), wts['w_branch'][l],
                   wts['w_out'][l], min(MERGE_TM, n))

    def cols(name, width=BR_W):
        off = BLK512[name] * BR_W
        return proj3[:, :, off:off + width]
    kv = _kv_layout(proj3, slot, depth, kv, min(KV_TM, t))
    last = proj3[:, t - 1:, :]
    a_off = BLK512['a_r'] * BR_W
    shift = jnp.concatenate([last[:, :, a_off:a_off + 3 * BR_W],
                             last[:, :, LORA_BLK * LANES:LORA_BLK * LANES + 2 * LORA]], axis=-1)
    c_off = BLK512['c_q'] * BR_W
    conv = proj3[:, t - (CONV_W - 1):, c_off:c_off + 3 * BR_W]
    mk = mem_kv[:, :, :BR_W].reshape(bsz, -1, H_X, HD_X)
    mv = mem_kv[:, :, BR_W:].reshape(bsz, -1, H_X, HD_X)
    return x_new.reshape(bsz, t, d), (s_rwkv, shift, s_gdn, conv, mk, mv), kv


def _sample_layer(x, caches, p, wts, l):
    n, _, d = x.shape
    (cache_k, cache_v, page_table, cache_mk, cache_mv, st_rwkv, st_shift, st_gdn, st_conv) = caches
    rwkv_prm, gdn_prm = _layer_params(p, l)
    lam_init = 0.8 - 0.6 * math.exp(-0.3 * l)
    proj = _norm_matmul(x.reshape(n, d), p['norm_g'][l], wts['w_in'][l], n, PROJ_TN)
    proj3 = proj.reshape(n, 1, PROJ_W)
    sh = st_shift[l].reshape(n, -1)
    shift_in = (sh[:, :BR_W], sh[:, BR_W:2 * BR_W], sh[:, 2 * BR_W:3 * BR_W], sh[:, 3 * BR_W:])
    rows = _rwkv_prep_sample(proj, shift_in, rwkv_prm)
    o_a, s_rwkv = _rwkv_step(rows, p['rwkv_ln_g'][l], p['rwkv_ln_b'][l], p['rwkv_rk'][l], st_rwkv, l, 8)
    o_b = _diff_decode(proj3, cache_k, cache_v, page_table, p['diff_lam'][l], p['diff_norm_g'][l], lam_init, l)
    buf = jnp.moveaxis(st_conv[l], 1, 0)
    bufs = (buf[:, :, :BR_W], buf[:, :, BR_W:2 * BR_W], buf[:, :, 2 * BR_W:])
    gq, gk, gv, eg, beta = _gdn_prep_sample(proj, bufs, gdn_prm)
    z_off = BLK512['c_z'] * BR_W
    o_c, s_gdn = _gdn_step(gq, gk, gv, eg, beta, proj[:, z_off:z_off + BR_W], p['gdn_norm_g'][l], st_gdn, l, 4)
    o_x = _cross_decode(proj3, cache_mk, cache_mv, l, 4)
    x_new = _merge(x.reshape(n, d), proj, o_a, o_b.reshape(n, BR_W), o_c, o_x.reshape(n, BR_W), wts['w_branch'][l],
                   wts['w_out'][l], n)

    def cols(name, width=BR_W):
        off = BLK512[name] * BR_W
        return proj[:, off:off + width]
    k_new = cols('b_k').reshape(n, 1, H_B, 2 * DK_B)
    v_new = cols('b_v').reshape(n, 1, H_B, DV_B)
    shift = jnp.concatenate([cols('a_r', 3 * BR_W), proj[:, LORA_BLK * LANES:LORA_BLK * LANES + 2 * LORA]],
                            axis=-1).reshape(n, 1, -1)
    conv = jnp.concatenate([st_conv[l][:, 1:], cols('c_q', 3 * BR_W).reshape(n, 1, -1)], axis=1)
    return x_new.reshape(n, 1, d), (k_new, v_new, s_rwkv, shift, s_gdn, conv)


def kernel(x_prompt, x_sample, mem_prompt, cache_diff_k, cache_diff_v, page_table, cache_mem_k, cache_mem_v,
           state_rwkv, state_rwkv_shift, state_gdn, state_gdn_conv, norm_g, w_in, rwkv_mu, rwkv_w0, rwkv_w2,
           rwkv_a0, rwkv_a2, rwkv_kk, rwkv_ka, rwkv_rk, rwkv_ln_g, rwkv_ln_b, diff_lam, diff_norm_g, gdn_conv_w,
           gdn_a_log, gdn_dt_bias, gdn_norm_g, norm_mem_g, w_mem_kv, w_branch, w_out, final_norm_g):
    p = {'norm_g': norm_g, 'rwkv_mu': rwkv_mu, 'rwkv_w0': rwkv_w0, 'rwkv_w2': rwkv_w2, 'rwkv_a0': rwkv_a0,
         'rwkv_a2': rwkv_a2, 'rwkv_kk': rwkv_kk, 'rwkv_ka': rwkv_ka, 'rwkv_rk': rwkv_rk, 'rwkv_ln_g': rwkv_ln_g,
         'rwkv_ln_b': rwkv_ln_b, 'diff_lam': diff_lam, 'diff_norm_g': diff_norm_g, 'gdn_conv_w': gdn_conv_w,
         'gdn_a_log': gdn_a_log, 'gdn_dt_bias': gdn_dt_bias, 'gdn_norm_g': gdn_norm_g, 'norm_mem_g': norm_mem_g}
    depth = w_in.shape[0]
    wts = {'w_in': _permute_w_in(w_in), 'w_mem': w_mem_kv.astype(BF16), 'w_branch': w_branch.astype(BF16),
           'w_out': w_out.astype(BF16)}
    n_s = x_sample.shape[0]
    caches = (_merge_token_head(cache_diff_k), _merge_token_head(cache_diff_v), page_table,
              _merge_token_head(cache_mem_k), _merge_token_head(cache_mem_v), state_rwkv, state_rwkv_shift, state_gdn,
              state_gdn_conv)
    xp, xs = x_prompt, x_sample
    outs_p, outs_s = [], []
    kv = None
    for l in range(depth):
        xp, st_p, kv = _prompt_layer(xp, mem_prompt, p, wts, l, kv, l, depth)
        xs, st_s = _sample_layer(xs, caches, p, wts, l)
        outs_p.append(st_p)
        outs_s.append(st_s)
    bsz, t, d = xp.shape
    y_prompt = _final_norm(xp.reshape(bsz * t, d), final_norm_g, min(512, bsz * t)).reshape(bsz, t, d)
    y_sample = _final_norm(xs.reshape(n_s, d), final_norm_g, n_s).reshape(n_s, 1, d)
    diff_k_prompt, diff_v_prompt = (z.reshape(depth, bsz, t, H_B, DV_B) for z in kv)
    sp = lambda i: jnp.stack([o[i] for o in outs_p])
    ss = lambda i: jnp.stack([o[i] for o in outs_s])
    return (y_prompt, y_sample, diff_k_prompt, diff_v_prompt, ss(0), ss(1), sp(4), sp(5), sp(0), ss(2), sp(1), ss(3),
            sp(2), ss(4), sp(3), ss(5))
```

```python
import functools
import math

import jax
import jax.numpy as jnp
from jax import lax
from jax.experimental import pallas as pl
from jax.experimental.pallas import tpu as pltpu

F32 = jnp.float32
BF16 = jnp.bfloat16
HI = lax.Precision.HIGHEST

D_MODEL = 1024
BR_W = D_MODEL // 2
N_BRANCH = 4
N_A = 64
H_A = BR_W // N_A
LORA = 64
H_B = 4
DK_B = 64
DV_B = 128
H_C = 4
DK_C = 128
DV_C = 128
CONV_W = 4
CHUNK = 64
H_X = 4
HD_X = BR_W // H_X
PAGE = 128
RWKV_GN_EPS = 64e-5
NORM_EPS = 1e-6
NEG_INF = -1e30
LANES = 128
SUBLANES = 8
VMEM_LIMIT = 48 * 1024 * 1024
PROJ_TM = 1024
PROJ_TN = 1280
MERGE_TM = 256
REDUCE_WAYS = 8
KV_TM = 512
RWKV_NCH = 2
GDN_NCH = 2
FLASH_TQ = 512
FLASH_COLS = 256

W512 = ('a_r', 'a_k', 'a_v', 'a_z', 'b_q', 'b_k', 'b_v', 'b_z', 'c_q', 'c_k', 'c_v', 'c_z', 'x_q', 'x_z')
MERGE_W = N_BRANCH * D_MODEL
BLK512 = {name: MERGE_W // BR_W + i for i, name in enumerate(W512)}
LORA_BLK = (MERGE_W + len(W512) * BR_W) // LANES
CAB_BLK = LORA_BLK + 1
PROJ_W = (CAB_BLK + 1) * LANES
REF_OFF = {'a_r': 0, 'a_k': 512, 'a_v': 1024, 'a_lora': 1536, 'a_z': 1664, 'b_q': 2176, 'b_k': 2688, 'b_v': 3200,
           'b_z': 3712, 'c_q': 4224, 'c_k': 4736, 'c_v': 5248, 'c_ab': 5760, 'c_z': 5768, 'x_q': 6280, 'x_z': 6792,
           'merge': 7304}


def _cparams(*sem):
    return pltpu.CompilerParams(dimension_semantics=sem, vmem_limit_bytes=VMEM_LIMIT)


def _sigmoid(x):
    return 1.0 / (1.0 + jnp.exp(-x))


def _silu(x):
    return x * _sigmoid(x)


def _softplus(x):
    return jnp.maximum(x, 0.0) + jnp.log(1.0 + jnp.exp(-jnp.abs(x)))


def _dot(a, b, precision=None):
    return jnp.dot(a, b, preferred_element_type=F32, precision=precision)


def _dot_nt(a, b, precision=None):
    return lax.dot_general(a, b, (((1,), (1,)), ((), ())), preferred_element_type=F32, precision=precision)


def _iota(shape, axis):
    return lax.broadcasted_iota(jnp.int32, shape, axis)


def _split(x):
    hi = x.astype(BF16)
    return hi, (x - hi.astype(F32)).astype(BF16)


def _seg_allsum64(xs):
    same = (_iota((LANES, LANES), 0) // N_A) == (_iota((LANES, LANES), 1) // N_A)
    ones = jnp.where(same, 1.0, 0.0).astype(BF16)
    ones2 = jnp.concatenate([ones, ones], axis=0)
    return [_dot(jnp.concatenate(_split(x), axis=1), ones2) for x in xs]


def _lhs3(x):
    hi, lo = _split(x)
    return jnp.concatenate([hi, hi, lo], axis=1)


def _rhs3(x):
    hi, lo = _split(x)
    return jnp.concatenate([hi, lo, hi], axis=0)


def _bdot(a, b):
    return _dot(a.astype(BF16), b.astype(BF16))


def _bdot_nt(a, b):
    return _dot_nt(a.astype(BF16), b.astype(BF16))


def _tri_inverse(ls, split):
    n = ls[0].shape[0]
    eye = (_iota((n, n), 0) == _iota((n, n), 1)).astype(F32)
    lhs, rhs = (_lhs3, _rhs3) if split else ((lambda z: z.astype(BF16)),) * 2
    xs = [eye + l for l in ls]
    ps = [_dot(lhs(l), rhs(l)) for l in ls]
    for _ in range(4):
        xps = [_dot(lhs(jnp.concatenate([x, p], axis=0)), rhs(p)) for x, p in zip(xs, ps)]
        xs = [x + xp[:n] for x, xp in zip(xs, xps)]
        ps = [xp[n:] for xp in xps]
    return [x + _dot(lhs(x), rhs(p)) for x, p in zip(xs, ps)]


def _norm_matmul_kernel(x_ref, g_ref, w_ref, o_ref, h_ref):
    @pl.when(pl.program_id(1) == 0)
    def _():
        x = x_ref[...]
        y = x * lax.rsqrt(jnp.mean(x * x, axis=-1, keepdims=True) + NORM_EPS)
        h_ref[...] = (y * g_ref[...]).astype(BF16)
    o_ref[...] = _dot(h_ref[...], w_ref[...])


def _norm_matmul(x, g, w, tm, tn):
    n, d = x.shape
    c = w.shape[1]
    return pl.pallas_call(
        _norm_matmul_kernel,
        grid=(n // tm, c // tn),
        in_specs=[pl.BlockSpec((tm, d), lambda i, j: (i, 0)),
                  pl.BlockSpec((1, d), lambda i, j: (0, 0)),
                  pl.BlockSpec((d, tn), lambda i, j: (0, j))],
        out_specs=pl.BlockSpec((tm, tn), lambda i, j: (i, j)),
        out_shape=jax.ShapeDtypeStruct((n, c), F32),
        scratch_shapes=[pltpu.VMEM((tm, d), BF16)],
        compiler_params=_cparams("parallel", "arbitrary"),
    )(x, g.reshape(1, d), w)


def _final_norm_kernel(x_ref, g_ref, o_ref):
    x = x_ref[...]
    o_ref[...] = x * lax.rsqrt(jnp.mean(x * x, axis=-1, keepdims=True) + NORM_EPS) * g_ref[...]


def _final_norm(x, g, tm):
    n, d = x.shape
    return pl.pallas_call(
        _final_norm_kernel,
        grid=(n // tm,),
        in_specs=[pl.BlockSpec((tm, d), lambda i: (i, 0)), pl.BlockSpec((1, d), lambda i: (0, 0))],
        out_specs=pl.BlockSpec((tm, d), lambda i: (i, 0)),
        out_shape=jax.ShapeDtypeStruct((n, d), F32),
        compiler_params=_cparams("parallel"),
    )(x, g.reshape(1, d))


def _rwkv_prep_math(cur, prev, prm):
    mu_r, mu_k, mu_v, mu_l, w0, w2p, a0, a2p, kkp, kap = prm
    r = cur[0] + (prev[0] - cur[0]) * mu_r
    k = cur[1] + (prev[1] - cur[1]) * mu_k
    v = cur[2] + (prev[2] - cur[2]) * mu_v
    lo = cur[3] + (prev[3] - cur[3]) * mu_l
    w_log = -_softplus(-(w0 + _dot(jnp.tanh(lo), w2p))) - 0.5
    log_decay = -jnp.exp(w_log)
    a = _sigmoid(a0 + _dot(lo, a2p))
    kk = k * kkp
    kk2 = kk * kk
    ssq = _seg_allsum64([kk2[:, i * LANES:(i + 1) * LANES] for i in range(BR_W // LANES)])
    kk = kk * lax.rsqrt(jnp.concatenate(ssq, axis=1) + 1e-6)
    k = k * (1.0 + (a - 1.0) * kap)
    return r, log_decay, k, v, -kk, kk * a


def _rwkv_prep_sample_kernel(pr, pk, pv, plo, qr, qk, qv, qlo, mu_r, mu_k, mu_v, mu_l, w0, w2p, a0, a2p, kkp, kap,
                             o_r, o_w, o_k, o_v, o_a, o_b):
    cur = (pr[...], pk[...], pv[...], plo[...])
    prev = (qr[...], qk[...], qv[...], qlo[...])
    prm = tuple(z[...] for z in (mu_r, mu_k, mu_v, mu_l, w0, w2p, a0, a2p, kkp, kap))
    outs = _rwkv_prep_math(cur, prev, prm)
    for o, val in zip((o_r, o_w, o_k, o_v, o_a, o_b), outs):
        o[...] = val


def _full_spec(a):
    nd = a.ndim
    return pl.BlockSpec(a.shape, lambda *_: (0,) * nd)


def _rwkv_prep_sample(proj, shift, prm):
    n = proj.shape[0]

    def col(blk, w):
        return pl.BlockSpec((n, w), lambda i: (0, blk))
    out = pl.BlockSpec((n, BR_W), lambda i: (0, 0))
    return pl.pallas_call(
        _rwkv_prep_sample_kernel,
        grid=(1,),
        in_specs=[col(BLK512['a_r'], BR_W), col(BLK512['a_k'], BR_W), col(BLK512['a_v'], BR_W), col(LORA_BLK, LANES)]
                 + [_full_spec(s) for s in shift] + [_full_spec(p) for p in prm],
        out_specs=[out] * 6,
        out_shape=[jax.ShapeDtypeStruct((n, BR_W), F32)] * 6,
        compiler_params=_cparams("arbitrary"),
    )(proj, proj, proj, proj, *shift, *prm)


def _stack_pair(x):
    lane = _iota(x.shape, 1)
    return jnp.concatenate([jnp.where(lane < N_A, x, 0.0), jnp.where(lane >= N_A, x, 0.0)], axis=0)


def _rwkv_group_norm_bonus(ys, rs, ks, vs, gs, bs, rks):
    n = ys[0].shape[0]
    sums = _seg_allsum64([jnp.concatenate([y, r * k * rk], axis=0) for y, r, k, rk in zip(ys, rs, ks, rks)])
    ds = [y - s[:n] * (1.0 / N_A) for y, s in zip(ys, sums)]
    vars_ = _seg_allsum64([d * d for d in ds])
    return [d * lax.rsqrt(var * (1.0 / N_A) + RWKV_GN_EPS) * g + b + s[n:] * v
            for d, var, g, b, s, v in zip(ds, vars_, gs, bs, sums, vs)]


def _rwkv_chunk_kernel(pr, pk, pv, plo, mu_r, mu_k, mu_v, mu_l, w0, w2p, a0, a2p, kkp, kap, g_ref, bb_ref, rk_ref,
                       o_ref, s_ref, st_ref, c_r, c_k, c_v, c_l):
    ci = pl.program_id(1)
    nc = pl.num_programs(1)
    c = CHUNK
    carries = (c_r, c_k, c_v, c_l)

    @pl.when(ci == 0)
    def _():
        st_ref[...] = jnp.zeros_like(st_ref)
        for cr in carries:
            cr[...] = jnp.zeros_like(cr)
    cur = (pr[...], pk[...], pv[...], plo[...])
    rows = cur[0].shape[0]
    prev = [jnp.where(_iota(z.shape, 0) == 0, cr[...], pltpu.roll(z, 1, 0)) for z, cr in zip(cur, carries)]
    for z, cr in zip(cur, carries):
        cr[...] = z[rows - 1:rows, :]
    prm = tuple(z[...] for z in (mu_r, mu_k, mu_v, mu_l, w0, w2p, a0, a2p, kkp, kap))
    r, lw, k, v, a_in, b_in = _rwkv_prep_math(cur, prev, prm)
    ti = _iota((rows, rows), 0)
    tj = _iota((rows, rows), 1)
    tri = ((ti >= tj) & ((ti // c) == (tj // c))).astype(F32)
    cum = _dot(tri, lw, HI)
    p_incl = jnp.exp(cum)
    p_inv = jnp.exp(-cum)
    p_prev = jnp.exp(cum - lw)
    at = a_in * p_prev
    bt = b_in * p_inv
    kt = k * p_inv
    rt = r * p_incl
    n2 = 2 * c
    ri = _iota((n2, n2), 0)
    cj = _iota((n2, n2), 1)
    same = (ri >= c) == (cj >= c)
    strict = same & ((ri & (c - 1)) > (cj & (c - 1)))
    incl = same & ((ri & (c - 1)) >= (cj & (c - 1)))
    eye = (ri == cj).astype(F32)
    n_ch = rows // c
    n_pair = H_A // 2
    units = [(ch, p) for ch in range(n_ch) for p in range(n_pair)]

    def rsl(ch):
        return slice(ch * c, (ch + 1) * c)

    def lsl(p):
        return slice(p * LANES, (p + 1) * LANES)
    stk = {name: [_stack_pair(z[rsl(ch), lsl(p)]) for ch, p in units]
           for name, z in (('a', at), ('b', bt), ('k', kt), ('r', rt), ('v', v))}
    prods = [_dot_nt(jnp.concatenate([a_s, r_s], axis=0).astype(BF16),
                     jnp.concatenate([b_s, k_s], axis=0).astype(BF16))
             for a_s, r_s, b_s, k_s in zip(stk['a'], stk['r'], stk['b'], stk['k'])]
    l_ab = [jnp.where(strict, pr[:n2, :n2], 0.0) for pr in prods]
    l_ak = [jnp.where(strict, pr[:n2, n2:], 0.0).astype(BF16) for pr in prods]
    m_r = [jnp.concatenate([jnp.where(incl, pr[n2:, :n2], 0.0), jnp.where(incl, pr[n2:, n2:], 0.0)],
                           axis=1).astype(BF16) for pr in prods]
    v_b = [z.astype(BF16) for z in stk['v']]
    lv = [_dot(l, z) for l, z in zip(l_ak, v_b)]
    t_inv = _tri_inverse(l_ab, split=False)
    wu = [_bdot(t, jnp.concatenate([a_s, z], axis=1)) for t, a_s, z in zip(t_inv, stk['a'], lv)]
    p_last = [p_incl[(ch + 1) * c - 1:(ch + 1) * c, lsl(p)] for ch, p in units]
    bk_t = [jnp.concatenate([(b_s * pl_).T, (k_s * pl_).T], axis=1).astype(BF16)
            for b_s, k_s, pl_ in zip(stk['b'], stk['k'], p_last)]
    p_col = [jnp.sum(eye * pl_, axis=1, keepdims=True) for pl_ in p_last]
    r_b = [z.astype(BF16) for z in stk['r']]
    s_t = [st_ref[p] for p in range(n_pair)]
    for ch in range(n_ch):
        ix = [ch * n_pair + p for p in range(n_pair)]
        s_b = [s.astype(BF16) for s in s_t]
        u_b = [(_dot(wu[i][:, :LANES].astype(BF16), s) + wu[i][:, LANES:]).astype(BF16) for i, s in zip(ix, s_b)]
        y_s = [_dot(jnp.concatenate([r_b[i], m_r[i]], axis=1), jnp.concatenate([s, u, v_b[i]], axis=0))
               for i, s, u in zip(ix, s_b, u_b)]
        s_t = [s * p_col[i] + _dot(bk_t[i], jnp.concatenate([u, v_b[i]], axis=0)) for i, s, u in zip(ix, s_t, u_b)]
        rs = rsl(ch)
        pairs = [lsl(p) for p in range(n_pair)]
        outs = _rwkv_group_norm_bonus([z[:c] + z[c:] for z in y_s], [r[rs, sl] for sl in pairs],
                                      [k[rs, sl] for sl in pairs], [v[rs, sl] for sl in pairs],
                                      [g_ref[:, sl] for sl in pairs], [bb_ref[:, sl] for sl in pairs],
                                      [rk_ref[:, sl] for sl in pairs])
        for sl, out in zip(pairs, outs):
            o_ref[rs, sl] = out
    for p in range(n_pair):
        st_ref[p] = s_t[p]

    @pl.when(ci == nc - 1)
    def _():
        for p in range(H_A // 2):
            s_vk = st_ref[p].T
            s_ref[2 * p] = s_vk[:N_A, :N_A]
            s_ref[2 * p + 1] = s_vk[N_A:, N_A:]


def _rwkv_chunk(proj, prm, ln_g, ln_b, rk):
    bsz, t, _ = proj.shape
    rows = min(RWKV_NCH * CHUNK, t)

    def col(blk, w):
        return pl.BlockSpec((None, rows, w), lambda bi, ci: (bi, ci, blk))
    par = pl.BlockSpec((1, BR_W), lambda bi, ci: (0, 0))
    return pl.pallas_call(
        _rwkv_chunk_kernel,
        grid=(bsz, t // rows),
        in_specs=[col(BLK512['a_r'], BR_W), col(BLK512['a_k'], BR_W), col(BLK512['a_v'], BR_W), col(LORA_BLK, LANES)]
                 + [_full_spec(z) for z in prm] + [par] * 3,
        out_specs=[pl.BlockSpec((None, rows, BR_W), lambda bi, ci: (bi, ci, 0)),
                   pl.BlockSpec((None, H_A, N_A, N_A), lambda bi, ci: (bi, 0, 0, 0))],
        out_shape=[jax.ShapeDtypeStruct((bsz, t, BR_W), F32), jax.ShapeDtypeStruct((bsz, H_A, N_A, N_A), F32)],
        scratch_shapes=[pltpu.VMEM((H_A // 2, LANES, LANES), F32)] + [pltpu.VMEM((1, BR_W), F32)] * 3
                       + [pltpu.VMEM((1, LANES), F32)],
        compiler_params=_cparams("parallel", "arbitrary"),
    )(proj, proj, proj, proj, *prm, ln_g, ln_b, rk)


def _rwkv_step_kernel(r_ref, w_ref, k_ref, v_ref, a_ref, b_ref, g_ref, bb_ref, rk_ref, s_ref, prev_ref, o_ref, so_ref):
    del prev_ref
    nb = s_ref.shape[0]
    eye = (_iota((N_A, N_A), 0) == _iota((N_A, N_A), 1)).astype(F32)

    def body(bi, carry):
        for h in range(H_A):
            s = s_ref[bi, h]
            r, k, v = r_ref[bi, h], k_ref[bi, h], v_ref[bi, h]
            sa = jnp.sum(s * a_ref[bi, h], axis=1, keepdims=True)
            v_col = jnp.sum(eye * v, axis=1, keepdims=True)
            s2 = s * jnp.exp(w_ref[bi, h]) + sa * b_ref[bi, h] + v_col * k
            y_col = jnp.sum(s2 * r, axis=1, keepdims=True)
            y = jnp.sum(eye * y_col, axis=0, keepdims=True)
            mu = jnp.mean(y, axis=1, keepdims=True)
            d = y - mu
            var = jnp.mean(d * d, axis=1, keepdims=True)
            yn = d * lax.rsqrt(var + RWKV_GN_EPS) * g_ref[h] + bb_ref[h]
            o_ref[bi, h] = yn + jnp.sum(r * k * rk_ref[h], axis=1, keepdims=True) * v
            so_ref[bi, h] = s2
        return carry
    lax.fori_loop(0, nb, body, 0)


def _layer_slot_spec(state, layer, nb):
    return pl.BlockSpec((None, nb) + state.shape[2:], lambda i: (layer, i) + (0,) * (state.ndim - 2))


def _rwkv_step(rows, ln_g, ln_b, rk, state, layer, nb, prev):
    n = rows[0].shape[0]
    rows = [z.reshape(n, H_A, 1, N_A) for z in rows]
    prm = [z.reshape(H_A, 1, N_A) for z in (ln_g, ln_b, rk)]
    row = pl.BlockSpec((nb, H_A, 1, N_A), lambda i: (i, 0, 0, 0))
    par = pl.BlockSpec((H_A, 1, N_A), lambda i: (0, 0, 0))
    st = _layer_slot_spec(state, layer, nb)
    o, s_new = pl.pallas_call(
        _rwkv_step_kernel,
        grid=(n // nb,),
        in_specs=[row] * 6 + [par] * 3 + [st, pl.BlockSpec(memory_space=pl.ANY)],
        out_specs=[row, st],
        out_shape=[jax.ShapeDtypeStruct((n, H_A, 1, N_A), F32), jax.ShapeDtypeStruct(prev.shape, prev.dtype)],
        input_output_aliases={10: 1},
        compiler_params=_cparams("parallel"),
    )(*rows, *prm, state, prev)
    return o.reshape(n, BR_W), s_new


def _gdn_norm_qk(q, k):
    qs, ks = [], []
    for h in range(H_C):
        sl = slice(h * DK_C, (h + 1) * DK_C)
        qh, kh = q[:, sl], k[:, sl]
        qs.append(qh * lax.rsqrt(jnp.sum(qh * qh, axis=1, keepdims=True) + 1e-6) * (DK_C ** -0.5))
        ks.append(kh * lax.rsqrt(jnp.sum(kh * kh, axis=1, keepdims=True) + 1e-6))
    return jnp.concatenate(qs, axis=1), jnp.concatenate(ks, axis=1)


def _gdn_gate(ab, alog, dtb):
    lane = _iota(ab.shape, 1)
    return jnp.where(lane < H_C, -jnp.exp(alog) * _softplus(ab + dtb), _sigmoid(ab))


def _gdn_causal_conv(x, w_ref, tail_ref):
    tm = x.shape[0]
    tail = tail_ref[...]
    acc = x * w_ref[CONV_W - 1:CONV_W, :]
    row8 = _iota(tail.shape, 0)
    for s in range(1, CONV_W):
        rolled = pltpu.roll(x, s, 0)
        head = jnp.where(row8 < s, pltpu.roll(tail, s, 0), rolled[:SUBLANES])
        xs = jnp.concatenate([head, rolled[SUBLANES:]], axis=0)
        acc = acc + xs * w_ref[CONV_W - 1 - s:CONV_W - s, :]
    tail_ref[...] = x[tm - SUBLANES:, :]
    return _silu(acc)


def _gdn_prep_sample_kernel(xq, xk, xv, xab, bq, bk, bv, wq, wk, wv, alog, dtb, o_q, o_k, o_v, o_eg, o_beta):
    outs = []
    for x_ref, b_ref, w_ref in zip((xq, xk, xv), (bq, bk, bv), (wq, wk, wv)):
        acc = x_ref[...] * w_ref[CONV_W - 1:CONV_W, :]
        for i in range(CONV_W - 1):
            acc = acc + b_ref[i] * w_ref[i:i + 1, :]
        outs.append(_silu(acc))
    q, k = _gdn_norm_qk(outs[0], outs[1])
    o_q[...] = q
    o_k[...] = k
    o_v[...] = outs[2]
    gb = _gdn_gate(xab[...], alog[...], dtb[...])
    n = gb.shape[0]
    eg = jnp.exp(gb)
    o_eg[...] = jnp.concatenate([jnp.broadcast_to(eg[:, h:h + 1], (n, DV_C)) for h in range(H_C)], axis=1)
    o_beta[...] = jnp.concatenate([jnp.broadcast_to(gb[:, H_C + h:H_C + h + 1], (n, DV_C)) for h in range(H_C)],
                                  axis=1)


def _gdn_prep_sample(proj, bufs, prm):
    n = proj.shape[0]

    def col(blk, w):
        return pl.BlockSpec((n, w), lambda i: (0, blk))
    out = pl.BlockSpec((n, BR_W), lambda i: (0, 0))
    return pl.pallas_call(
        _gdn_prep_sample_kernel,
        grid=(1,),
        in_specs=[col(BLK512['c_q'], BR_W), col(BLK512['c_k'], BR_W), col(BLK512['c_v'], BR_W), col(CAB_BLK, LANES)]
                 + [_full_spec(z) for z in bufs] + [_full_spec(p) for p in prm],
        out_specs=[out] * 5,
        out_shape=[jax.ShapeDtypeStruct((n, BR_W), F32)] * 5,
        compiler_params=_cparams("arbitrary"),
    )(proj, proj, proj, proj, *bufs, *prm)


def _gdn_out_norm(o, g, z):
    return o * lax.rsqrt(jnp.mean(o * o, axis=1, keepdims=True) + NORM_EPS) * g * _silu(z)


def _gdn_chunk_kernel(xq, xk, xv, xab, z_ref, wq, wk, wv, alog, dtb, g_ref, o_ref, s_ref, st_ref, t_q, t_k, t_v):
    ci = pl.program_id(1)
    nc = pl.num_programs(1)
    c = CHUNK
    rows = xab.shape[0]
    n_ch = rows // c

    @pl.when(ci == 0)
    def _():
        st_ref[...] = jnp.zeros_like(st_ref)
        for tl in (t_q, t_k, t_v):
            tl[...] = jnp.zeros_like(tl)
    q_all, k_all = _gdn_norm_qk(_gdn_causal_conv(xq[...], wq, t_q), _gdn_causal_conv(xk[...], wk, t_k))
    v_all = _gdn_causal_conv(xv[...], wv, t_v)
    gb = _gdn_gate(xab[...], alog[...], dtb[...])
    ri = _iota((rows, rows), 0)
    cj = _iota((rows, rows), 1)
    same = (ri // c) == (cj // c)
    incl = same & (ri >= cj)
    strict = same & (ri > cj)
    cum = _dot(incl.astype(F32), gb, HI)
    cum_t = cum.T
    heads = range(H_C)
    hsl = [slice(h * DK_C, (h + 1) * DK_C) for h in heads]
    gc_col = [cum[:, h:h + 1] for h in heads]
    beta = [gb[:, H_C + h:H_C + h + 1] for h in heads]
    decay = [jnp.where(incl, jnp.exp(jnp.where(incl, gc_col[h] - cum_t[h:h + 1, :], 0.0)), 0.0) for h in heads]
    q = [q_all[:, sl] for sl in hsl]
    k = [k_all[:, sl] for sl in hsl]
    kb = [k[h] * beta[h] for h in heads]
    prod = [_dot_nt(jnp.concatenate([kb[h], q[h]], axis=0).astype(BF16), k[h].astype(BF16)) for h in heads]
    t_inv = _tri_inverse([-jnp.where(strict, prod[h][:rows] * decay[h], 0.0) for h in heads], split=True)
    qk = [jnp.where(incl, prod[h][rows:] * decay[h], 0.0) for h in heads]
    uw = [_bdot(t_inv[h], jnp.concatenate([v_all[:, hsl[h]] * beta[h], kb[h] * jnp.exp(gc_col[h])], axis=1))
          for h in heads]
    qg = [q[h] * jnp.exp(gc_col[h]) for h in heads]
    s = [st_ref[h] for h in heads]
    for ch in range(n_ch):
        rs = slice(ch * c, (ch + 1) * c)
        g_last = [cum[(ch + 1) * c - 1:(ch + 1) * c, h:h + 1] for h in heads]
        s_b = [z.astype(BF16) for z in s]
        v_nb = [(uw[h][rs, :DV_C] - _dot(uw[h][rs, DV_C:].astype(BF16), s_b[h])).astype(BF16) for h in heads]
        zero = jnp.zeros_like(v_nb[0])
        o = [_dot(jnp.concatenate([qg[h][rs], qk[h][rs]], axis=1).astype(BF16),
                  jnp.concatenate([s_b[h]] + [v_nb[h] if i == ch else zero for i in range(n_ch)], axis=0))
             for h in heads]
        s = [s[h] * jnp.exp(g_last[h]) + _dot((k[h][rs] * jnp.exp(g_last[h] - gc_col[h][rs])).T.astype(BF16), v_nb[h])
             for h in heads]
        for h in heads:
            o_ref[rs, hsl[h]] = _gdn_out_norm(o[h], g_ref[...], z_ref[rs, hsl[h]])
    for h in heads:
        st_ref[h] = s[h]

    @pl.when(ci == nc - 1)
    def _():
        s_ref[...] = st_ref[...]


def _gdn_chunk(proj, prm, norm_g):
    bsz, t, _ = proj.shape
    rows = min(GDN_NCH * CHUNK, t)

    def col(blk, w):
        return pl.BlockSpec((None, rows, w), lambda bi, ci: (bi, ci, blk))
    return pl.pallas_call(
        _gdn_chunk_kernel,
        grid=(bsz, t // rows),
        in_specs=[col(BLK512['c_q'], BR_W), col(BLK512['c_k'], BR_W), col(BLK512['c_v'], BR_W), col(CAB_BLK, LANES),
                  col(BLK512['c_z'], BR_W)] + [_full_spec(z) for z in prm]
                 + [pl.BlockSpec((1, DV_C), lambda bi, ci: (0, 0))],
        out_specs=[pl.BlockSpec((None, rows, BR_W), lambda bi, ci: (bi, ci, 0)),
                   pl.BlockSpec((None, H_C, DK_C, DV_C), lambda bi, ci: (bi, 0, 0, 0))],
        out_shape=[jax.ShapeDtypeStruct((bsz, t, BR_W), F32), jax.ShapeDtypeStruct((bsz, H_C, DK_C, DV_C), F32)],
        scratch_shapes=[pltpu.VMEM((H_C, DK_C, DV_C), F32)] + [pltpu.VMEM((SUBLANES, BR_W), F32)] * 3,
        compiler_params=_cparams("parallel", "arbitrary"),
    )(proj, proj, proj, proj, proj, *prm, norm_g.reshape(1, DV_C))


def _gdn_step_kernel(q_ref, k_ref, v_ref, eg_ref, beta_ref, z_ref, g_ref, s_ref, prev_ref, o_ref, so_ref):
    del prev_ref
    nb = s_ref.shape[0]
    eye = (_iota((DK_C, DK_C), 0) == _iota((DK_C, DK_C), 1)).astype(F32)

    def body(bi, carry):
        for h in range(H_C):
            s = s_ref[bi, h]
            eg = eg_ref[bi, h]
            k_col = jnp.sum(eye * k_ref[bi, h], axis=1, keepdims=True)
            q_col = jnp.sum(eye * q_ref[bi, h], axis=1, keepdims=True)
            ks = jnp.sum(s * k_col, axis=0, keepdims=True)
            u = beta_ref[bi, h] * (v_ref[bi, h] - eg * ks)
            s2 = s * eg + k_col * u
            o = jnp.sum(s2 * q_col, axis=0, keepdims=True)
            o_ref[bi, h] = _gdn_out_norm(o, g_ref[...], z_ref[bi, h])
            so_ref[bi, h] = s2
        return carry
    lax.fori_loop(0, nb, body, 0)


def _gdn_step(q, k, v, eg, beta, z, norm_g, state, layer, nb, prev):
    n = q.shape[0]
    rows = [x.reshape(n, H_C, 1, DV_C) for x in (q, k, v, eg, beta, z)]
    row = pl.BlockSpec((nb, H_C, 1, DV_C), lambda i: (i, 0, 0, 0))
    st = _layer_slot_spec(state, layer, nb)
    o, s_new = pl.pallas_call(
        _gdn_step_kernel,
        grid=(n // nb,),
        in_specs=[row] * 6 + [pl.BlockSpec((1, DV_C), lambda i: (0, 0)), st, pl.BlockSpec(memory_space=pl.ANY)],
        out_specs=[row, st],
        out_shape=[jax.ShapeDtypeStruct((n, H_C, 1, DV_C), F32), jax.ShapeDtypeStruct(prev.shape, prev.dtype)],
        input_output_aliases={8: 1},
        compiler_params=_cparams("parallel"),
    )(*rows, norm_g.reshape(1, DV_C), state, prev)
    return o.reshape(n, BR_W), s_new


def _diff_lambda(lam_ref, lam_init):
    lv = lam_ref[...]
    s1 = jnp.sum(lv[0:1] * lv[1:2], axis=1, keepdims=True)
    s2 = jnp.sum(lv[2:3] * lv[3:4], axis=1, keepdims=True)
    return jnp.exp(s1) - jnp.exp(s2) + lam_init


def _diff_out_norm(o, g, lam_init):
    return o * lax.rsqrt(jnp.mean(o * o, axis=1, keepdims=True) + 1e-5) * g * (1.0 - lam_init)


def _row_reduce(x, op):
    rows, cols = x.shape
    part = op(x.reshape(REDUCE_WAYS, rows // (REDUCE_WAYS * SUBLANES), SUBLANES, cols), axis=1)
    return op(op(part, axis=0), axis=0, keepdims=True)


def _diff_flash_kernel(q_ref, k_ref, v_ref, lam_ref, g_ref, o_ref, qs_ref, m_ref, l_ref, acc_ref, s_ref, *, lam_init):
    i = pl.program_id(2)
    tq = q_ref.shape[0]
    q = q_ref[...] * (DK_B ** -0.5)
    lane = _iota(q.shape, 1)
    qs_ref[:tq] = jnp.where(lane < DK_B, q, 0.0).astype(BF16)
    qs_ref[tq:] = jnp.where(lane >= DK_B, q, 0.0).astype(BF16)
    m_ref[...] = jnp.full_like(m_ref, NEG_INF)
    l_ref[...] = jnp.zeros_like(l_ref)
    acc_ref[...] = jnp.zeros_like(acc_ref)

    n_grp = 2 * tq // FLASH_COLS
    grp = [slice(g * FLASH_COLS, (g + 1) * FLASH_COLS) for g in range(n_grp)]

    def kv_rows(j):
        return pl.ds(pl.multiple_of(j * tq, tq), tq)

    def scores(j):
        s_ref[j % 2] = _dot_nt(k_ref[kv_rows(j), :].astype(BF16), qs_ref[...])

    def consume(j, diagonal):
        v_t = v_ref[kv_rows(j), :].T.astype(BF16)
        slot = j % 2
        s = [s_ref[slot, :, sl] for sl in grp]
        if diagonal:
            row = _iota(s[0].shape, 0)
            col = _iota(s[0].shape, 1)
            s = [jnp.where(row <= ((col + g * FLASH_COLS) & (tq - 1)), z, NEG_INF) for g, z in enumerate(s)]
        m_old = [m_ref[:, sl] for sl in grp]
        m_new = [jnp.maximum(mo, _row_reduce(z, jnp.max)) for mo, z in zip(m_old, s)]
        p = [jnp.exp(z - mn) for z, mn in zip(s, m_new)]
        pv = [_dot(v_t, z.astype(BF16)) for z in p]
        for sl, mo, mn, z, zv in zip(grp, m_old, m_new, p, pv):
            alpha = jnp.exp(mo - mn)
            l_ref[:, sl] = alpha * l_ref[:, sl] + _row_reduce(z, jnp.sum)
            acc_ref[:, sl] = alpha * acc_ref[:, sl] + zv
            m_ref[:, sl] = mn

    scores(0)

    def body(j, carry):
        consume(j - 1, False)
        scores(j)
        return carry
    lax.fori_loop(1, i + 1, body, 0)
    consume(i, True)
    lam = _diff_lambda(lam_ref, lam_init)
    on = acc_ref[...] / l_ref[...]
    o = (on[:, :tq] - lam * on[:, tq:]).T
    o_ref[...] = _diff_out_norm(o, g_ref[...], lam_init)


def _diff_flash(proj, lam, norm_g, lam_init, tq):
    bsz, t, _ = proj.shape
    kq, kk, kv = (BLK512[n] * (BR_W // LANES) for n in ('b_q', 'b_k', 'b_v'))
    return pl.pallas_call(
        functools.partial(_diff_flash_kernel, lam_init=lam_init),
        grid=(bsz, H_B, t // tq),
        in_specs=[pl.BlockSpec((None, tq, LANES), lambda b, h, i: (b, i, kq + h)),
                  pl.BlockSpec((None, t, LANES), lambda b, h, i: (b, 0, kk + h)),
                  pl.BlockSpec((None, t, LANES), lambda b, h, i: (b, 0, kv + h)),
                  pl.BlockSpec((4, DK_B), lambda b, h, i: (0, 0)),
                  pl.BlockSpec((1, DV_B), lambda b, h, i: (0, 0))],
        out_specs=pl.BlockSpec((None, tq, LANES), lambda b, h, i: (b, i, h)),
        out_shape=jax.ShapeDtypeStruct((bsz, t, BR_W), F32),
        scratch_shapes=[pltpu.VMEM((2 * tq, LANES), BF16), pltpu.VMEM((1, 2 * tq), F32),
                        pltpu.VMEM((1, 2 * tq), F32), pltpu.VMEM((DV_B, 2 * tq), F32),
                        pltpu.VMEM((2, tq, 2 * tq), F32)],
        compiler_params=_cparams("parallel", "parallel", "arbitrary"),
    )(proj, proj, proj, lam, norm_g.reshape(1, DV_B))


def _head_rows(x, n_rep):
    rows = []
    for h in range(x.shape[1] // LANES):
        rows += [x[:, h * LANES:(h + 1) * LANES]] * n_rep
    return jnp.concatenate(rows, axis=0)


def _diff_decode_kernel(pt_ref, q_ref, kn_ref, vn_ref, *rest, lam_init, npg):
    del pt_ref
    k_refs, v_refs = rest[:npg], rest[npg:2 * npg]
    lam_ref, g_ref, o_ref = rest[2 * npg:]
    nrow = 2 * H_B
    q8 = _head_rows(q_ref[...] * (DK_B ** -0.5), 2)
    lane = _iota(q8.shape, 1)
    row = _iota(q8.shape, 0)
    q8 = jnp.where((lane >= DK_B) == ((row & 1) == 1), q8, 0.0)
    q8b = q8.astype(BF16)
    ncol = k_refs[0].shape[0]
    valid = (_iota((nrow, ncol), 1) & (H_B - 1)) == (_iota((nrow, ncol), 0) >> 1)
    s_new = jnp.sum(q8 * _head_rows(kn_ref[...], 2), axis=1, keepdims=True)
    s_pages = [jnp.where(valid, _dot_nt(q8b, k[...].astype(BF16)), NEG_INF) for k in k_refs]
    m = s_new
    for s in s_pages:
        m = jnp.maximum(m, jnp.max(s, axis=1, keepdims=True))
    p_new = jnp.exp(s_new - m)
    l_sum = p_new
    acc = p_new * _head_rows(vn_ref[...], 2)
    for s, v in zip(s_pages, v_refs):
        p = jnp.exp(s - m)
        l_sum = l_sum + jnp.sum(p, axis=1, keepdims=True)
        acc = acc + _dot(p.astype(BF16), v[...].astype(BF16))
    on = acc / l_sum
    lam = _diff_lambda(lam_ref, lam_init)
    for h in range(H_B):
        o = on[2 * h:2 * h + 1] - lam * on[2 * h + 1:2 * h + 2]
        o_ref[:, h * DV_B:(h + 1) * DV_B] = _diff_out_norm(o, g_ref[...], lam_init)


def _diff_decode(proj, cache_k, cache_v, page_table, lam, norm_g, lam_init, layer):
    n = proj.shape[0]
    npg = page_table.shape[1]

    def col(name):
        return pl.BlockSpec((None, 1, BR_W), lambda b, pt: (b, 0, BLK512[name]))

    def page(i):
        return pl.BlockSpec((None, None, PAGE * H_B, DV_B), lambda b, pt: (layer, pt[b * npg + i], 0, 0))
    pages = [page(i) for i in range(npg)]
    grid_spec = pltpu.PrefetchScalarGridSpec(
        num_scalar_prefetch=1,
        grid=(n,),
        in_specs=[col('b_q'), col('b_k'), col('b_v')] + pages + pages
                 + [pl.BlockSpec((4, DK_B), lambda b, pt: (0, 0)), pl.BlockSpec((1, DV_B), lambda b, pt: (0, 0))],
        out_specs=pl.BlockSpec((None, 1, BR_W), lambda b, pt: (b, 0, 0)))
    return pl.pallas_call(
        functools.partial(_diff_decode_kernel, lam_init=lam_init, npg=npg),
        grid_spec=grid_spec,
        out_shape=jax.ShapeDtypeStruct((n, 1, BR_W), F32),
        compiler_params=_cparams("parallel"),
    )(page_table.reshape(-1), proj, proj, proj, *([cache_k] * npg), *([cache_v] * npg), lam,
      norm_g.reshape(1, DV_B))


def _cross_kernel(q_ref, k_ref, v_ref, o_ref):
    hsl = [slice(h * HD_X, (h + 1) * HD_X) for h in range(H_X)]
    s = [_dot_nt(q_ref[:, sl].astype(BF16), k_ref[:, sl].astype(BF16)) * (HD_X ** -0.5) for sl in hsl]
    p = [jnp.exp(z - jnp.max(z, axis=1, keepdims=True)) for z in s]
    o = [_dot(z.astype(BF16), v_ref[:, sl].astype(BF16)) for z, sl in zip(p, hsl)]
    for sl, z, oz in zip(hsl, p, o):
        o_ref[:, sl] = oz / jnp.sum(z, axis=1, keepdims=True)


def _cross_prompt(proj, mem_kv, tq):
    bsz, t, _ = proj.shape
    n_mem = mem_kv.shape[1]
    return pl.pallas_call(
        _cross_kernel,
        grid=(bsz, t // tq),
        in_specs=[pl.BlockSpec((None, tq, BR_W), lambda b, i: (b, i, BLK512['x_q'])),
                  pl.BlockSpec((None, n_mem, BR_W), lambda b, i: (b, 0, 0)),
                  pl.BlockSpec((None, n_mem, BR_W), lambda b, i: (b, 0, 1))],
        out_specs=pl.BlockSpec((None, tq, BR_W), lambda b, i: (b, i, 0)),
        out_shape=jax.ShapeDtypeStruct((bsz, t, BR_W), F32),
        compiler_params=_cparams("parallel", "parallel"),
    )(proj, mem_kv, mem_kv)


def _cross_decode_kernel(q_ref, k_ref, v_ref, o_ref):
    nb = q_ref.shape[0]
    ncol = k_ref.shape[1]
    valid = (_iota((SUBLANES, ncol), 1) & (H_X - 1)) == _iota((SUBLANES, ncol), 0)
    for b in range(nb):
        q8 = jnp.concatenate([_head_rows(q_ref[b], 1), jnp.zeros((SUBLANES - H_X, HD_X), F32)], axis=0)
        s = _dot_nt(q8.astype(BF16), k_ref[b].astype(BF16)) * (HD_X ** -0.5)
        s = jnp.where(valid, s, NEG_INF)
        m = jnp.max(s, axis=1, keepdims=True)
        p = jnp.exp(s - m)
        o = _dot(p.astype(BF16), v_ref[b].astype(BF16)) / jnp.sum(p, axis=1, keepdims=True)
        for h in range(H_X):
            o_ref[b, :, h * HD_X:(h + 1) * HD_X] = o[h:h + 1]


def _cross_decode(proj, cache_k, cache_v, layer, nb):
    n = proj.shape[0]
    rows = cache_k.shape[2]
    kv = pl.BlockSpec((None, nb, rows, HD_X), lambda i: (layer, i, 0, 0))
    return pl.pallas_call(
        _cross_decode_kernel,
        grid=(n // nb,),
        in_specs=[pl.BlockSpec((nb, 1, BR_W), lambda i: (i, 0, BLK512['x_q'])), kv, kv],
        out_specs=pl.BlockSpec((nb, 1, BR_W), lambda i: (i, 0, 0)),
        out_shape=jax.ShapeDtypeStruct((n, 1, BR_W), F32),
        compiler_params=_cparams("parallel"),
    )(proj, cache_k, cache_v)


def _merge_kernel(x_ref, oa_ref, az_ref, ob_ref, bz_ref, oc_ref, ox_ref, xz_ref, mg_ref, wb_ref, wo_ref, o_ref):
    branches = (oa_ref[...] * _silu(az_ref[...]), ob_ref[...] * _silu(bz_ref[...]), oc_ref[...],
                ox_ref[...] * _silu(xz_ref[...]))
    acc = None
    for n, br in enumerate(branches):
        gate = _sigmoid(mg_ref[:, n * D_MODEL:(n + 1) * D_MODEL])
        term = gate * _dot(br.astype(BF16), wb_ref[n])
        acc = term if acc is None else acc + term
    o_ref[...] = x_ref[...] + _dot(acc.astype(BF16), wo_ref[...])


def _merge(x, proj, o_a, o_b, o_c, o_x, w_branch, w_out, tm):
    n = x.shape[0]

    def col(name):
        return pl.BlockSpec((tm, BR_W), lambda i: (i, BLK512[name]))
    br = pl.BlockSpec((tm, BR_W), lambda i: (i, 0))
    return pl.pallas_call(
        _merge_kernel,
        grid=(n // tm,),
        in_specs=[pl.BlockSpec((tm, D_MODEL), lambda i: (i, 0)),
                  br, col('a_z'), br, col('b_z'), br, br, col('x_z'),
                  pl.BlockSpec((tm, MERGE_W), lambda i: (i, 0)),
                  pl.BlockSpec((N_BRANCH, BR_W, D_MODEL), lambda i: (0, 0, 0)),
                  pl.BlockSpec((D_MODEL, D_MODEL), lambda i: (0, 0))],
        out_specs=pl.BlockSpec((tm, D_MODEL), lambda i: (i, 0)),
        out_shape=jax.ShapeDtypeStruct((n, D_MODEL), F32),
        compiler_params=_cparams("parallel"),
    )(x, o_a, proj, o_b, proj, o_c, o_x, proj, proj, w_branch, w_out)


def _kv_layout_kernel(*refs):
    k_ref, v_ref, ok_ref, ov_ref = refs[0], refs[1], refs[-2], refs[-1]
    tm = k_ref.shape[0]
    for src, dst in ((k_ref, ok_ref), (v_ref, ov_ref)):
        for h in range(H_B):
            dst[pl.ds(h, tm, stride=H_B), :] = src[:, h * LANES:(h + 1) * LANES]


def _kv_layout(proj, slot, prev, tm):
    bsz, t, _ = proj.shape
    out = pl.BlockSpec((None, None, tm * H_B, LANES), lambda b, i: (slot, b, i, 0))
    return pl.pallas_call(
        _kv_layout_kernel,
        grid=(bsz, t // tm),
        in_specs=[pl.BlockSpec((None, tm, BR_W), lambda b, i: (b, i, BLK512['b_k'])),
                  pl.BlockSpec((None, tm, BR_W), lambda b, i: (b, i, BLK512['b_v'])),
                  pl.BlockSpec(memory_space=pl.ANY), pl.BlockSpec(memory_space=pl.ANY)],
        out_specs=[out, out],
        out_shape=[jax.ShapeDtypeStruct(z.shape, z.dtype) for z in prev],
        input_output_aliases={2: 0, 3: 1},
        compiler_params=_cparams("parallel", "parallel"),
    )(proj, proj, *prev)


def _merge_token_head(cache):
    return cache.reshape(*cache.shape[:-3], cache.shape[-3] * cache.shape[-2], cache.shape[-1])


def _permute_w_in(w_in):
    d = w_in.shape[0]
    parts = [w_in[..., REF_OFF['merge']:REF_OFF['merge'] + MERGE_W]]
    parts += [w_in[..., REF_OFF[n]:REF_OFF[n] + BR_W] for n in W512]
    parts.append(w_in[..., REF_OFF['a_lora']:REF_OFF['a_lora'] + 2 * LORA])
    parts.append(w_in[..., REF_OFF['c_ab']:REF_OFF['c_ab'] + 2 * H_C])
    parts.append(jnp.zeros(w_in.shape[:-1] + (LANES - 2 * H_C,), w_in.dtype))
    del d
    return jnp.concatenate(parts, axis=-1).astype(BF16)


def _layer_params(p, l):
    mu = p['rwkv_mu'][l]
    row = lambda z: z.reshape(1, -1)
    zeros64 = jnp.zeros((LORA, BR_W), F32)
    rwkv = (row(mu[:BR_W]), row(mu[BR_W:2 * BR_W]), row(mu[2 * BR_W:3 * BR_W]), row(mu[3 * BR_W:]),
            row(p['rwkv_w0'][l]), jnp.concatenate([p['rwkv_w2'][l], zeros64], axis=0),
            row(p['rwkv_a0'][l]), jnp.concatenate([zeros64, p['rwkv_a2'][l]], axis=0),
            row(p['rwkv_kk'][l]), row(p['rwkv_ka'][l]))
    wc = p['gdn_conv_w'][l]
    pad = lambda z: jnp.pad(z, (0, LANES - z.shape[0])).reshape(1, LANES)
    gdn = (wc[:, :BR_W], wc[:, BR_W:2 * BR_W], wc[:, 2 * BR_W:], pad(p['gdn_a_log'][l]), pad(p['gdn_dt_bias'][l]))
    return rwkv, gdn


def _prompt_layer(x, mem, p, wts, l, kv, slot):
    bsz, t, d = x.shape
    n = bsz * t
    rwkv_prm, gdn_prm = _layer_params(p, l)
    lam_init = 0.8 - 0.6 * math.exp(-0.3 * l)
    mem_kv = _norm_matmul(mem.reshape(-1, d), p['norm_mem_g'][l], wts['w_mem'][l], 256, 512)
    mem_kv = mem_kv.reshape(bsz, -1, 2 * BR_W)
    proj = _norm_matmul(x.reshape(n, d), p['norm_g'][l], wts['w_in'][l], min(PROJ_TM, n), PROJ_TN)
    proj3 = proj.reshape(bsz, t, PROJ_W)
    row = lambda z: z.reshape(1, -1)
    o_a, s_rwkv = _rwkv_chunk(proj3, rwkv_prm, row(p['rwkv_ln_g'][l]), row(p['rwkv_ln_b'][l]), row(p['rwkv_rk'][l]))
    o_b = _diff_flash(proj3, p['diff_lam'][l], p['diff_norm_g'][l], lam_init, min(FLASH_TQ, t // 2))
    o_c, s_gdn = _gdn_chunk(proj3, gdn_prm, p['gdn_norm_g'][l])
    o_x = _cross_prompt(proj3, mem_kv, min(512, t))
    flat = lambda z: z.reshape(n, BR_W)
    x_new = _merge(x.reshape(n, d), proj, flat(o_a), flat(o_b), flat(o_c), flat(o_x), wts['w_branch'][l],
                   wts['w_out'][l], min(MERGE_TM, n))

    def cols(name, width=BR_W):
        off = BLK512[name] * BR_W
        return proj3[:, :, off:off + width]
    kv = _kv_layout(proj3, slot, kv, min(KV_TM, t))
    last = proj3[:, t - 1:, :]
    a_off = BLK512['a_r'] * BR_W
    shift = jnp.concatenate([last[:, :, a_off:a_off + 3 * BR_W],
                             last[:, :, LORA_BLK * LANES:LORA_BLK * LANES + 2 * LORA]], axis=-1)
    c_off = BLK512['c_q'] * BR_W
    conv = proj3[:, t - (CONV_W - 1):, c_off:c_off + 3 * BR_W]
    mk = mem_kv[:, :, :BR_W].reshape(bsz, -1, H_X, HD_X)
    mv = mem_kv[:, :, BR_W:].reshape(bsz, -1, H_X, HD_X)
    return x_new.reshape(bsz, t, d), (s_rwkv, shift, s_gdn, conv, mk, mv), kv


def _sample_layer(x, caches, p, wts, l, prev):
    n, _, d = x.shape
    (cache_k, cache_v, page_table, cache_mk, cache_mv, st_rwkv, st_shift, st_gdn, st_conv) = caches
    rwkv_prm, gdn_prm = _layer_params(p, l)
    lam_init = 0.8 - 0.6 * math.exp(-0.3 * l)
    proj = _norm_matmul(x.reshape(n, d), p['norm_g'][l], wts['w_in'][l], n, PROJ_TN)
    proj3 = proj.reshape(n, 1, PROJ_W)
    sh = st_shift[l].reshape(n, -1)
    shift_in = (sh[:, :BR_W], sh[:, BR_W:2 * BR_W], sh[:, 2 * BR_W:3 * BR_W], sh[:, 3 * BR_W:])
    rows = _rwkv_prep_sample(proj, shift_in, rwkv_prm)
    o_a, s_rwkv = _rwkv_step(rows, p['rwkv_ln_g'][l], p['rwkv_ln_b'][l], p['rwkv_rk'][l], st_rwkv, l, 8, prev[0])
    o_b = _diff_decode(proj3, cache_k, cache_v, page_table, p['diff_lam'][l], p['diff_norm_g'][l], lam_init, l)
    buf = jnp.moveaxis(st_conv[l], 1, 0)
    bufs = (buf[:, :, :BR_W], buf[:, :, BR_W:2 * BR_W], buf[:, :, 2 * BR_W:])
    gq, gk, gv, eg, beta = _gdn_prep_sample(proj, bufs, gdn_prm)
    z_off = BLK512['c_z'] * BR_W
    o_c, s_gdn = _gdn_step(gq, gk, gv, eg, beta, proj[:, z_off:z_off + BR_W], p['gdn_norm_g'][l], st_gdn, l, 4,
                           prev[1])
    o_x = _cross_decode(proj3, cache_mk, cache_mv, l, 4)
    x_new = _merge(x.reshape(n, d), proj, o_a, o_b.reshape(n, BR_W), o_c, o_x.reshape(n, BR_W), wts['w_branch'][l],
                   wts['w_out'][l], n)

    def cols(name, width=BR_W):
        off = BLK512[name] * BR_W
        return proj[:, off:off + width]
    k_new = cols('b_k').reshape(n, 1, H_B, 2 * DK_B)
    v_new = cols('b_v').reshape(n, 1, H_B, DV_B)
    shift = jnp.concatenate([cols('a_r', 3 * BR_W), proj[:, LORA_BLK * LANES:LORA_BLK * LANES + 2 * LORA]],
                            axis=-1).reshape(n, 1, -1)
    conv = jnp.concatenate([st_conv[l][:, 1:], cols('c_q', 3 * BR_W).reshape(n, 1, -1)], axis=1)
    return x_new.reshape(n, 1, d), (k_new, v_new, s_rwkv, shift, s_gdn, conv)


def kernel(x_prompt, x_sample, mem_prompt, cache_diff_k, cache_diff_v, page_table, cache_mem_k, cache_mem_v,
           state_rwkv, state_rwkv_shift, state_gdn, state_gdn_conv, norm_g, w_in, rwkv_mu, rwkv_w0, rwkv_w2,
           rwkv_a0, rwkv_a2, rwkv_kk, rwkv_ka, rwkv_rk, rwkv_ln_g, rwkv_ln_b, diff_lam, diff_norm_g, gdn_conv_w,
           gdn_a_log, gdn_dt_bias, gdn_norm_g, norm_mem_g, w_mem_kv, w_branch, w_out, final_norm_g):
    p = {'norm_g': norm_g, 'rwkv_mu': rwkv_mu, 'rwkv_w0': rwkv_w0, 'rwkv_w2': rwkv_w2, 'rwkv_a0': rwkv_a0,
         'rwkv_a2': rwkv_a2, 'rwkv_kk': rwkv_kk, 'rwkv_ka': rwkv_ka, 'rwkv_rk': rwkv_rk, 'rwkv_ln_g': rwkv_ln_g,
         'rwkv_ln_b': rwkv_ln_b, 'diff_lam': diff_lam, 'diff_norm_g': diff_norm_g, 'gdn_conv_w': gdn_conv_w,
         'gdn_a_log': gdn_a_log, 'gdn_dt_bias': gdn_dt_bias, 'gdn_norm_g': gdn_norm_g, 'norm_mem_g': norm_mem_g}
    depth = w_in.shape[0]
    wts = {'w_in': _permute_w_in(w_in), 'w_mem': w_mem_kv.astype(BF16), 'w_branch': w_branch.astype(BF16),
           'w_out': w_out.astype(BF16)}
    n_s = x_sample.shape[0]
    caches = (_merge_token_head(cache_diff_k), _merge_token_head(cache_diff_v), page_table,
              _merge_token_head(cache_mem_k), _merge_token_head(cache_mem_v), state_rwkv, state_rwkv_shift, state_gdn,
              state_gdn_conv)
    xp, xs = x_prompt, x_sample
    outs_p, outs_s = [], []
    bsz_p, t_p = x_prompt.shape[:2]
    kv = tuple(jnp.zeros((depth, bsz_p, t_p * H_B, LANES), F32) for _ in range(2))
    stacked = (jnp.zeros_like(state_rwkv), jnp.zeros_like(state_gdn))
    for l in range(depth):
        xp, st_p, kv = _prompt_layer(xp, mem_prompt, p, wts, l, kv, l)
        xs, st_s = _sample_layer(xs, caches, p, wts, l, stacked)
        stacked = (st_s[2], st_s[4])
        outs_p.append(st_p)
        outs_s.append(st_s)
    bsz, t, d = xp.shape
    y_prompt = _final_norm(xp.reshape(bsz * t, d), final_norm_g, min(512, bsz * t)).reshape(bsz, t, d)
    y_sample = _final_norm(xs.reshape(n_s, d), final_norm_g, n_s).reshape(n_s, 1, d)
    diff_k_prompt, diff_v_prompt = (z.reshape(depth, bsz, t, H_B, DV_B) for z in kv)
    sp = lambda i: jnp.stack([o[i] for o in outs_p])
    ss = lambda i: jnp.stack([o[i] for o in outs_s])
    return (y_prompt, y_sample, diff_k_prompt, diff_v_prompt, ss(0), ss(1), sp(4), sp(5), sp(0), stacked[0], sp(1),
            ss(3), sp(2), stacked[1], sp(3), ss(5))
```

```python
import functools
import math

import jax
import jax.numpy as jnp
from jax import lax
from jax.experimental import pallas as pl
from jax.experimental.pallas import tpu as pltpu

F32 = jnp.float32
BF16 = jnp.bfloat16
HI = lax.Precision.HIGHEST

D_MODEL = 1024
BR_W = D_MODEL // 2
N_BRANCH = 4
N_A = 64
H_A = BR_W // N_A
LORA = 64
H_B = 4
DK_B = 64
DV_B = 128
H_C = 4
DK_C = 128
DV_C = 128
CONV_W = 4
CHUNK = 64
H_X = 4
HD_X = BR_W // H_X
PAGE = 128
RWKV_GN_EPS = 64e-5
NORM_EPS = 1e-6
NEG_INF = -1e30
LANES = 128
SUBLANES = 8
VMEM_LIMIT = 48 * 1024 * 1024
PROJ_TM = 1024
PROJ_TN = 1280
MERGE_TM = 256
REDUCE_WAYS = 8
KV_TM = 512
RWKV_NCH = 2
GDN_NCH = 2
FLASH_TQ = 512
FLASH_COLS = 256

W512 = ('a_r', 'a_k', 'a_v', 'a_z', 'b_q', 'b_k', 'b_v', 'b_z', 'c_q', 'c_k', 'c_v', 'c_z', 'x_q', 'x_z')
MERGE_W = N_BRANCH * D_MODEL
BLK512 = {name: MERGE_W // BR_W + i for i, name in enumerate(W512)}
LORA_BLK = (MERGE_W + len(W512) * BR_W) // LANES
CAB_BLK = LORA_BLK + 1
PROJ_W = (CAB_BLK + 1) * LANES
REF_OFF = {'a_r': 0, 'a_k': 512, 'a_v': 1024, 'a_lora': 1536, 'a_z': 1664, 'b_q': 2176, 'b_k': 2688, 'b_v': 3200,
           'b_z': 3712, 'c_q': 4224, 'c_k': 4736, 'c_v': 5248, 'c_ab': 5760, 'c_z': 5768, 'x_q': 6280, 'x_z': 6792,
           'merge': 7304}


def _cparams(*sem):
    return pltpu.CompilerParams(dimension_semantics=sem, vmem_limit_bytes=VMEM_LIMIT)


def _sigmoid(x):
    return 1.0 / (1.0 + jnp.exp(-x))


def _silu(x):
    return x * _sigmoid(x)


def _softplus(x):
    return jnp.maximum(x, 0.0) + jnp.log(1.0 + jnp.exp(-jnp.abs(x)))


def _dot(a, b, precision=None):
    return jnp.dot(a, b, preferred_element_type=F32, precision=precision)


def _dot_nt(a, b, precision=None):
    return lax.dot_general(a, b, (((1,), (1,)), ((), ())), preferred_element_type=F32, precision=precision)


def _iota(shape, axis):
    return lax.broadcasted_iota(jnp.int32, shape, axis)


def _split(x):
    hi = x.astype(BF16)
    return hi, (x - hi.astype(F32)).astype(BF16)


def _seg_allsum64(xs):
    same = (_iota((LANES, LANES), 0) // N_A) == (_iota((LANES, LANES), 1) // N_A)
    ones = jnp.where(same, 1.0, 0.0).astype(BF16)
    ones2 = jnp.concatenate([ones, ones], axis=0)
    return [_dot(jnp.concatenate(_split(x), axis=1), ones2) for x in xs]


def _lhs3(x):
    hi, lo = _split(x)
    return jnp.concatenate([hi, hi, lo], axis=1)


def _rhs3(x):
    hi, lo = _split(x)
    return jnp.concatenate([hi, lo, hi], axis=0)


def _bdot(a, b):
    return _dot(a.astype(BF16), b.astype(BF16))


def _bdot_nt(a, b):
    return _dot_nt(a.astype(BF16), b.astype(BF16))


def _tri_inverse(ls, split):
    n = ls[0].shape[0]
    eye = (_iota((n, n), 0) == _iota((n, n), 1)).astype(F32)
    lhs, rhs = (_lhs3, _rhs3) if split else ((lambda z: z.astype(BF16)),) * 2
    xs = [eye + l for l in ls]
    ps = [_dot(lhs(l), rhs(l)) for l in ls]
    for _ in range(4):
        xps = [_dot(lhs(jnp.concatenate([x, p], axis=0)), rhs(p)) for x, p in zip(xs, ps)]
        xs = [x + xp[:n] for x, xp in zip(xs, xps)]
        ps = [xp[n:] for xp in xps]
    return [x + _dot(lhs(x), rhs(p)) for x, p in zip(xs, ps)]


def _norm_matmul_kernel(x_ref, g_ref, w_ref, o_ref, h_ref):
    @pl.when(pl.program_id(1) == 0)
    def _():
        x = x_ref[...]
        y = x * lax.rsqrt(jnp.mean(x * x, axis=-1, keepdims=True) + NORM_EPS)
        h_ref[...] = (y * g_ref[...]).astype(BF16)
    o_ref[...] = _dot(h_ref[...], w_ref[...])


def _norm_matmul(x, g, w, tm, tn):
    n, d = x.shape
    c = w.shape[1]
    return pl.pallas_call(
        _norm_matmul_kernel,
        grid=(n // tm, c // tn),
        in_specs=[pl.BlockSpec((tm, d), lambda i, j: (i, 0)),
                  pl.BlockSpec((1, d), lambda i, j: (0, 0)),
                  pl.BlockSpec((d, tn), lambda i, j: (0, j))],
        out_specs=pl.BlockSpec((tm, tn), lambda i, j: (i, j)),
        out_shape=jax.ShapeDtypeStruct((n, c), F32),
        scratch_shapes=[pltpu.VMEM((tm, d), BF16)],
        compiler_params=_cparams("parallel", "arbitrary"),
    )(x, g.reshape(1, d), w)


def _final_norm_kernel(x_ref, g_ref, o_ref):
    x = x_ref[...]
    o_ref[...] = x * lax.rsqrt(jnp.mean(x * x, axis=-1, keepdims=True) + NORM_EPS) * g_ref[...]


def _final_norm(x, g, tm):
    n, d = x.shape
    return pl.pallas_call(
        _final_norm_kernel,
        grid=(n // tm,),
        in_specs=[pl.BlockSpec((tm, d), lambda i: (i, 0)), pl.BlockSpec((1, d), lambda i: (0, 0))],
        out_specs=pl.BlockSpec((tm, d), lambda i: (i, 0)),
        out_shape=jax.ShapeDtypeStruct((n, d), F32),
        compiler_params=_cparams("parallel"),
    )(x, g.reshape(1, d))


def _rwkv_prep_math(cur, prev, prm):
    mu_r, mu_k, mu_v, mu_l, w0, w2p, a0, a2p, kkp, kap = prm
    r = cur[0] + (prev[0] - cur[0]) * mu_r
    k = cur[1] + (prev[1] - cur[1]) * mu_k
    v = cur[2] + (prev[2] - cur[2]) * mu_v
    lo = cur[3] + (prev[3] - cur[3]) * mu_l
    w_log = -_softplus(-(w0 + _dot(jnp.tanh(lo), w2p))) - 0.5
    log_decay = -jnp.exp(w_log)
    a = _sigmoid(a0 + _dot(lo, a2p))
    kk = k * kkp
    kk2 = kk * kk
    ssq = _seg_allsum64([kk2[:, i * LANES:(i + 1) * LANES] for i in range(BR_W // LANES)])
    kk = kk * lax.rsqrt(jnp.concatenate(ssq, axis=1) + 1e-6)
    k = k * (1.0 + (a - 1.0) * kap)
    return r, log_decay, k, v, -kk, kk * a


def _rwkv_prep_sample_kernel(pr, pk, pv, plo, qr, qk, qv, qlo, mu_r, mu_k, mu_v, mu_l, w0, w2p, a0, a2p, kkp, kap,
                             o_r, o_w, o_k, o_v, o_a, o_b):
    cur = (pr[...], pk[...], pv[...], plo[...])
    prev = (qr[...], qk[...], qv[...], qlo[...])
    prm = tuple(z[...] for z in (mu_r, mu_k, mu_v, mu_l, w0, w2p, a0, a2p, kkp, kap))
    outs = _rwkv_prep_math(cur, prev, prm)
    for o, val in zip((o_r, o_w, o_k, o_v, o_a, o_b), outs):
        o[...] = val


def _full_spec(a):
    nd = a.ndim
    return pl.BlockSpec(a.shape, lambda *_: (0,) * nd)


def _rwkv_prep_sample(proj, shift, prm):
    n = proj.shape[0]

    def col(blk, w):
        return pl.BlockSpec((n, w), lambda i: (0, blk))
    out = pl.BlockSpec((n, BR_W), lambda i: (0, 0))
    return pl.pallas_call(
        _rwkv_prep_sample_kernel,
        grid=(1,),
        in_specs=[col(BLK512['a_r'], BR_W), col(BLK512['a_k'], BR_W), col(BLK512['a_v'], BR_W), col(LORA_BLK, LANES)]
                 + [_full_spec(s) for s in shift] + [_full_spec(p) for p in prm],
        out_specs=[out] * 6,
        out_shape=[jax.ShapeDtypeStruct((n, BR_W), F32)] * 6,
        compiler_params=_cparams("arbitrary"),
    )(proj, proj, proj, proj, *shift, *prm)


def _stack_pair(x):
    lane = _iota(x.shape, 1)
    return jnp.concatenate([jnp.where(lane < N_A, x, 0.0), jnp.where(lane >= N_A, x, 0.0)], axis=0)


def _rwkv_group_norm_bonus(ys, rs, ks, vs, gs, bs, rks):
    n = ys[0].shape[0]
    sums = _seg_allsum64([jnp.concatenate([y, r * k * rk], axis=0) for y, r, k, rk in zip(ys, rs, ks, rks)])
    ds = [y - s[:n] * (1.0 / N_A) for y, s in zip(ys, sums)]
    vars_ = _seg_allsum64([d * d for d in ds])
    return [d * lax.rsqrt(var * (1.0 / N_A) + RWKV_GN_EPS) * g + b + s[n:] * v
            for d, var, g, b, s, v in zip(ds, vars_, gs, bs, sums, vs)]


def _rwkv_chunk_kernel(pr, pk, pv, plo, mu_r, mu_k, mu_v, mu_l, w0, w2p, a0, a2p, kkp, kap, g_ref, bb_ref, rk_ref,
                       o_ref, s_ref, st_ref, c_r, c_k, c_v, c_l):
    ci = pl.program_id(1)
    nc = pl.num_programs(1)
    c = CHUNK
    carries = (c_r, c_k, c_v, c_l)

    @pl.when(ci == 0)
    def _():
        st_ref[...] = jnp.zeros_like(st_ref)
        for cr in carries:
            cr[...] = jnp.zeros_like(cr)
    cur = (pr[...], pk[...], pv[...], plo[...])
    rows = cur[0].shape[0]
    prev = [jnp.where(_iota(z.shape, 0) == 0, cr[...], pltpu.roll(z, 1, 0)) for z, cr in zip(cur, carries)]
    for z, cr in zip(cur, carries):
        cr[...] = z[rows - 1:rows, :]
    prm = tuple(z[...] for z in (mu_r, mu_k, mu_v, mu_l, w0, w2p, a0, a2p, kkp, kap))
    r, lw, k, v, a_in, b_in = _rwkv_prep_math(cur, prev, prm)
    ti = _iota((rows, rows), 0)
    tj = _iota((rows, rows), 1)
    tri = ((ti >= tj) & ((ti // c) == (tj // c))).astype(F32)
    cum = _dot(tri, lw, HI)
    p_incl = jnp.exp(cum)
    p_inv = jnp.exp(-cum)
    p_prev = jnp.exp(cum - lw)
    at = a_in * p_prev
    bt = b_in * p_inv
    kt = k * p_inv
    rt = r * p_incl
    n2 = 2 * c
    ri = _iota((n2, n2), 0)
    cj = _iota((n2, n2), 1)
    same = (ri >= c) == (cj >= c)
    strict = same & ((ri & (c - 1)) > (cj & (c - 1)))
    incl = same & ((ri & (c - 1)) >= (cj & (c - 1)))
    eye = (ri == cj).astype(F32)
    n_ch = rows // c
    n_pair = H_A // 2
    units = [(ch, p) for ch in range(n_ch) for p in range(n_pair)]

    def rsl(ch):
        return slice(ch * c, (ch + 1) * c)

    def lsl(p):
        return slice(p * LANES, (p + 1) * LANES)
    stk = {name: [_stack_pair(z[rsl(ch), lsl(p)]) for ch, p in units]
           for name, z in (('a', at), ('b', bt), ('k', kt), ('r', rt), ('v', v))}
    prods = [_dot_nt(jnp.concatenate([a_s, r_s], axis=0).astype(BF16),
                     jnp.concatenate([b_s, k_s], axis=0).astype(BF16))
             for a_s, r_s, b_s, k_s in zip(stk['a'], stk['r'], stk['b'], stk['k'])]
    l_ab = [jnp.where(strict, pr[:n2, :n2], 0.0) for pr in prods]
    l_ak = [jnp.where(strict, pr[:n2, n2:], 0.0).astype(BF16) for pr in prods]
    m_r = [jnp.concatenate([jnp.where(incl, pr[n2:, :n2], 0.0), jnp.where(incl, pr[n2:, n2:], 0.0)],
                           axis=1).astype(BF16) for pr in prods]
    v_b = [z.astype(BF16) for z in stk['v']]
    lv = [_dot(l, z) for l, z in zip(l_ak, v_b)]
    t_inv = _tri_inverse(l_ab, split=False)
    wu = [_bdot(t, jnp.concatenate([a_s, z], axis=1)) for t, a_s, z in zip(t_inv, stk['a'], lv)]
    p_last = [p_incl[(ch + 1) * c - 1:(ch + 1) * c, lsl(p)] for ch, p in units]
    bk_t = [jnp.concatenate([(b_s * pl_).T, (k_s * pl_).T], axis=1).astype(BF16)
            for b_s, k_s, pl_ in zip(stk['b'], stk['k'], p_last)]
    p_col = [jnp.sum(eye * pl_, axis=1, keepdims=True) for pl_ in p_last]
    r_b = [z.astype(BF16) for z in stk['r']]
    s_t = [st_ref[p] for p in range(n_pair)]
    for ch in range(n_ch):
        ix = [ch * n_pair + p for p in range(n_pair)]
        s_b = [s.astype(BF16) for s in s_t]
        u_b = [(_dot(wu[i][:, :LANES].astype(BF16), s) + wu[i][:, LANES:]).astype(BF16) for i, s in zip(ix, s_b)]
        y_s = [_dot(jnp.concatenate([r_b[i], m_r[i]], axis=1), jnp.concatenate([s, u, v_b[i]], axis=0))
               for i, s, u in zip(ix, s_b, u_b)]
        s_t = [s * p_col[i] + _dot(bk_t[i], jnp.concatenate([u, v_b[i]], axis=0)) for i, s, u in zip(ix, s_t, u_b)]
        rs = rsl(ch)
        pairs = [lsl(p) for p in range(n_pair)]
        outs = _rwkv_group_norm_bonus([z[:c] + z[c:] for z in y_s], [r[rs, sl] for sl in pairs],
                                      [k[rs, sl] for sl in pairs], [v[rs, sl] for sl in pairs],
                                      [g_ref[:, sl] for sl in pairs], [bb_ref[:, sl] for sl in pairs],
                                      [rk_ref[:, sl] for sl in pairs])
        for sl, out in zip(pairs, outs):
            o_ref[rs, sl] = out
    for p in range(n_pair):
        st_ref[p] = s_t[p]

    @pl.when(ci == nc - 1)
    def _():
        for p in range(H_A // 2):
            s_vk = st_ref[p].T
            s_ref[2 * p] = s_vk[:N_A, :N_A]
            s_ref[2 * p + 1] = s_vk[N_A:, N_A:]


def _rwkv_chunk(proj, prm, ln_g, ln_b, rk):
    bsz, t, _ = proj.shape
    rows = min(RWKV_NCH * CHUNK, t)

    def col(blk, w):
        return pl.BlockSpec((None, rows, w), lambda bi, ci: (bi, ci, blk))
    par = pl.BlockSpec((1, BR_W), lambda bi, ci: (0, 0))
    return pl.pallas_call(
        _rwkv_chunk_kernel,
        grid=(bsz, t // rows),
        in_specs=[col(BLK512['a_r'], BR_W), col(BLK512['a_k'], BR_W), col(BLK512['a_v'], BR_W), col(LORA_BLK, LANES)]
                 + [_full_spec(z) for z in prm] + [par] * 3,
        out_specs=[pl.BlockSpec((None, rows, BR_W), lambda bi, ci: (bi, ci, 0)),
                   pl.BlockSpec((None, H_A, N_A, N_A), lambda bi, ci: (bi, 0, 0, 0))],
        out_shape=[jax.ShapeDtypeStruct((bsz, t, BR_W), F32), jax.ShapeDtypeStruct((bsz, H_A, N_A, N_A), F32)],
        scratch_shapes=[pltpu.VMEM((H_A // 2, LANES, LANES), F32)] + [pltpu.VMEM((1, BR_W), F32)] * 3
                       + [pltpu.VMEM((1, LANES), F32)],
        compiler_params=_cparams("parallel", "arbitrary"),
    )(proj, proj, proj, proj, *prm, ln_g, ln_b, rk)


def _rwkv_step_kernel(r_ref, w_ref, k_ref, v_ref, a_ref, b_ref, g_ref, bb_ref, rk_ref, s_ref, prev_ref, o_ref, so_ref):
    del prev_ref
    nb = s_ref.shape[0]
    eye = (_iota((N_A, N_A), 0) == _iota((N_A, N_A), 1)).astype(F32)

    def body(bi, carry):
        for h in range(H_A):
            s = s_ref[bi, h]
            r, k, v = r_ref[bi, h], k_ref[bi, h], v_ref[bi, h]
            sa = jnp.sum(s * a_ref[bi, h], axis=1, keepdims=True)
            v_col = jnp.sum(eye * v, axis=1, keepdims=True)
            s2 = s * jnp.exp(w_ref[bi, h]) + sa * b_ref[bi, h] + v_col * k
            y_col = jnp.sum(s2 * r, axis=1, keepdims=True)
            y = jnp.sum(eye * y_col, axis=0, keepdims=True)
            mu = jnp.mean(y, axis=1, keepdims=True)
            d = y - mu
            var = jnp.mean(d * d, axis=1, keepdims=True)
            yn = d * lax.rsqrt(var + RWKV_GN_EPS) * g_ref[h] + bb_ref[h]
            o_ref[bi, h] = yn + jnp.sum(r * k * rk_ref[h], axis=1, keepdims=True) * v
            so_ref[bi, h] = s2
        return carry
    lax.fori_loop(0, nb, body, 0)


def _layer_slot_spec(state, layer, nb):
    return pl.BlockSpec((None, nb) + state.shape[2:], lambda i: (layer, i) + (0,) * (state.ndim - 2))


def _rwkv_step(rows, ln_g, ln_b, rk, state, layer, nb, prev):
    n = rows[0].shape[0]
    rows = [z.reshape(n, H_A, 1, N_A) for z in rows]
    prm = [z.reshape(H_A, 1, N_A) for z in (ln_g, ln_b, rk)]
    row = pl.BlockSpec((nb, H_A, 1, N_A), lambda i: (i, 0, 0, 0))
    par = pl.BlockSpec((H_A, 1, N_A), lambda i: (0, 0, 0))
    st = _layer_slot_spec(state, layer, nb)
    o, s_new = pl.pallas_call(
        _rwkv_step_kernel,
        grid=(n // nb,),
        in_specs=[row] * 6 + [par] * 3 + [st, pl.BlockSpec(memory_space=pl.ANY)],
        out_specs=[row, st],
        out_shape=[jax.ShapeDtypeStruct((n, H_A, 1, N_A), F32), jax.ShapeDtypeStruct(prev.shape, prev.dtype)],
        input_output_aliases={10: 1},
        compiler_params=_cparams("parallel"),
    )(*rows, *prm, state, prev)
    return o.reshape(n, BR_W), s_new


def _gdn_norm_qk(q, k):
    qs, ks = [], []
    for h in range(H_C):
        sl = slice(h * DK_C, (h + 1) * DK_C)
        qh, kh = q[:, sl], k[:, sl]
        qs.append(qh * lax.rsqrt(jnp.sum(qh * qh, axis=1, keepdims=True) + 1e-6) * (DK_C ** -0.5))
        ks.append(kh * lax.rsqrt(jnp.sum(kh * kh, axis=1, keepdims=True) + 1e-6))
    return jnp.concatenate(qs, axis=1), jnp.concatenate(ks, axis=1)


def _gdn_gate(ab, alog, dtb):
    lane = _iota(ab.shape, 1)
    return jnp.where(lane < H_C, -jnp.exp(alog) * _softplus(ab + dtb), _sigmoid(ab))


def _gdn_causal_conv(x, w_ref, tail_ref):
    tm = x.shape[0]
    tail = tail_ref[...]
    acc = x * w_ref[CONV_W - 1:CONV_W, :]
    row8 = _iota(tail.shape, 0)
    for s in range(1, CONV_W):
        rolled = pltpu.roll(x, s, 0)
        head = jnp.where(row8 < s, pltpu.roll(tail, s, 0), rolled[:SUBLANES])
        xs = jnp.concatenate([head, rolled[SUBLANES:]], axis=0)
        acc = acc + xs * w_ref[CONV_W - 1 - s:CONV_W - s, :]
    tail_ref[...] = x[tm - SUBLANES:, :]
    return _silu(acc)


def _gdn_prep_sample_kernel(xq, xk, xv, xab, bq, bk, bv, wq, wk, wv, alog, dtb, o_q, o_k, o_v, o_eg, o_beta):
    outs = []
    for x_ref, b_ref, w_ref in zip((xq, xk, xv), (bq, bk, bv), (wq, wk, wv)):
        acc = x_ref[...] * w_ref[CONV_W - 1:CONV_W, :]
        for i in range(CONV_W - 1):
            acc = acc + b_ref[i] * w_ref[i:i + 1, :]
        outs.append(_silu(acc))
    q, k = _gdn_norm_qk(outs[0], outs[1])
    o_q[...] = q
    o_k[...] = k
    o_v[...] = outs[2]
    gb = _gdn_gate(xab[...], alog[...], dtb[...])
    n = gb.shape[0]
    eg = jnp.exp(gb)
    o_eg[...] = jnp.concatenate([jnp.broadcast_to(eg[:, h:h + 1], (n, DV_C)) for h in range(H_C)], axis=1)
    o_beta[...] = jnp.concatenate([jnp.broadcast_to(gb[:, H_C + h:H_C + h + 1], (n, DV_C)) for h in range(H_C)],
                                  axis=1)


def _gdn_prep_sample(proj, bufs, prm):
    n = proj.shape[0]

    def col(blk, w):
        return pl.BlockSpec((n, w), lambda i: (0, blk))
    out = pl.BlockSpec((n, BR_W), lambda i: (0, 0))
    return pl.pallas_call(
        _gdn_prep_sample_kernel,
        grid=(1,),
        in_specs=[col(BLK512['c_q'], BR_W), col(BLK512['c_k'], BR_W), col(BLK512['c_v'], BR_W), col(CAB_BLK, LANES)]
                 + [_full_spec(z) for z in bufs] + [_full_spec(p) for p in prm],
        out_specs=[out] * 5,
        out_shape=[jax.ShapeDtypeStruct((n, BR_W), F32)] * 5,
        compiler_params=_cparams("arbitrary"),
    )(proj, proj, proj, proj, *bufs, *prm)


def _gdn_out_norm(o, g, z):
    return o * lax.rsqrt(jnp.mean(o * o, axis=1, keepdims=True) + NORM_EPS) * g * _silu(z)


def _gdn_chunk_kernel(xq, xk, xv, xab, z_ref, wq, wk, wv, alog, dtb, g_ref, o_ref, s_ref, st_ref, t_q, t_k, t_v):
    ci = pl.program_id(1)
    nc = pl.num_programs(1)
    c = CHUNK
    rows = xab.shape[0]
    n_ch = rows // c

    @pl.when(ci == 0)
    def _():
        st_ref[...] = jnp.zeros_like(st_ref)
        for tl in (t_q, t_k, t_v):
            tl[...] = jnp.zeros_like(tl)
    q_all, k_all = _gdn_norm_qk(_gdn_causal_conv(xq[...], wq, t_q), _gdn_causal_conv(xk[...], wk, t_k))
    v_all = _gdn_causal_conv(xv[...], wv, t_v)
    gb = _gdn_gate(xab[...], alog[...], dtb[...])
    ri = _iota((rows, rows), 0)
    cj = _iota((rows, rows), 1)
    same = (ri // c) == (cj // c)
    incl = same & (ri >= cj)
    strict = same & (ri > cj)
    cum = _dot(incl.astype(F32), gb, HI)
    cum_t = cum.T
    heads = range(H_C)
    hsl = [slice(h * DK_C, (h + 1) * DK_C) for h in heads]
    gc_col = [cum[:, h:h + 1] for h in heads]
    beta = [gb[:, H_C + h:H_C + h + 1] for h in heads]
    decay = [jnp.where(incl, jnp.exp(jnp.where(incl, gc_col[h] - cum_t[h:h + 1, :], 0.0)), 0.0) for h in heads]
    q = [q_all[:, sl] for sl in hsl]
    k = [k_all[:, sl] for sl in hsl]
    kb = [k[h] * beta[h] for h in heads]
    prod = [_dot_nt(jnp.concatenate([kb[h], q[h]], axis=0).astype(BF16), k[h].astype(BF16)) for h in heads]
    t_inv = _tri_inverse([-jnp.where(strict, prod[h][:rows] * decay[h], 0.0) for h in heads], split=True)
    qk = [jnp.where(incl, prod[h][rows:] * decay[h], 0.0) for h in heads]
    uw = [_bdot(t_inv[h], jnp.concatenate([v_all[:, hsl[h]] * beta[h], kb[h] * jnp.exp(gc_col[h])], axis=1))
          for h in heads]
    qg = [q[h] * jnp.exp(gc_col[h]) for h in heads]
    s = [st_ref[h] for h in heads]
    for ch in range(n_ch):
        rs = slice(ch * c, (ch + 1) * c)
        g_last = [cum[(ch + 1) * c - 1:(ch + 1) * c, h:h + 1] for h in heads]
        s_b = [z.astype(BF16) for z in s]
        v_nb = [(uw[h][rs, :DV_C] - _dot(uw[h][rs, DV_C:].astype(BF16), s_b[h])).astype(BF16) for h in heads]
        zero = jnp.zeros_like(v_nb[0])
        o = [_dot(jnp.concatenate([qg[h][rs], qk[h][rs]], axis=1).astype(BF16),
                  jnp.concatenate([s_b[h]] + [v_nb[h] if i == ch else zero for i in range(n_ch)], axis=0))
             for h in heads]
        s = [s[h] * jnp.exp(g_last[h]) + _dot((k[h][rs] * jnp.exp(g_last[h] - gc_col[h][rs])).T.astype(BF16), v_nb[h])
             for h in heads]
        for h in heads:
            o_ref[rs, hsl[h]] = _gdn_out_norm(o[h], g_ref[...], z_ref[rs, hsl[h]])
    for h in heads:
        st_ref[h] = s[h]

    @pl.when(ci == nc - 1)
    def _():
        s_ref[...] = st_ref[...]


def _gdn_chunk(proj, prm, norm_g):
    bsz, t, _ = proj.shape
    rows = min(GDN_NCH * CHUNK, t)

    def col(blk, w):
        return pl.BlockSpec((None, rows, w), lambda bi, ci: (bi, ci, blk))
    return pl.pallas_call(
        _gdn_chunk_kernel,
        grid=(bsz, t // rows),
        in_specs=[col(BLK512['c_q'], BR_W), col(BLK512['c_k'], BR_W), col(BLK512['c_v'], BR_W), col(CAB_BLK, LANES),
                  col(BLK512['c_z'], BR_W)] + [_full_spec(z) for z in prm]
                 + [pl.BlockSpec((1, DV_C), lambda bi, ci: (0, 0))],
        out_specs=[pl.BlockSpec((None, rows, BR_W), lambda bi, ci: (bi, ci, 0)),
                   pl.BlockSpec((None, H_C, DK_C, DV_C), lambda bi, ci: (bi, 0, 0, 0))],
        out_shape=[jax.ShapeDtypeStruct((bsz, t, BR_W), F32), jax.ShapeDtypeStruct((bsz, H_C, DK_C, DV_C), F32)],
        scratch_shapes=[pltpu.VMEM((H_C, DK_C, DV_C), F32)] + [pltpu.VMEM((SUBLANES, BR_W), F32)] * 3,
        compiler_params=_cparams("parallel", "arbitrary"),
    )(proj, proj, proj, proj, proj, *prm, norm_g.reshape(1, DV_C))


def _gdn_step_kernel(q_ref, k_ref, v_ref, eg_ref, beta_ref, z_ref, g_ref, s_ref, prev_ref, o_ref, so_ref):
    del prev_ref
    nb = s_ref.shape[0]
    eye = (_iota((DK_C, DK_C), 0) == _iota((DK_C, DK_C), 1)).astype(F32)

    def body(bi, carry):
        for h in range(H_C):
            s = s_ref[bi, h]
            eg = eg_ref[bi, h]
            k_col = jnp.sum(eye * k_ref[bi, h], axis=1, keepdims=True)
            q_col = jnp.sum(eye * q_ref[bi, h], axis=1, keepdims=True)
            ks = jnp.sum(s * k_col, axis=0, keepdims=True)
            u = beta_ref[bi, h] * (v_ref[bi, h] - eg * ks)
            s2 = s * eg + k_col * u
            o = jnp.sum(s2 * q_col, axis=0, keepdims=True)
            o_ref[bi, h] = _gdn_out_norm(o, g_ref[...], z_ref[bi, h])
            so_ref[bi, h] = s2
        return carry
    lax.fori_loop(0, nb, body, 0)


def _gdn_step(q, k, v, eg, beta, z, norm_g, state, layer, nb, prev):
    n = q.shape[0]
    rows = [x.reshape(n, H_C, 1, DV_C) for x in (q, k, v, eg, beta, z)]
    row = pl.BlockSpec((nb, H_C, 1, DV_C), lambda i: (i, 0, 0, 0))
    st = _layer_slot_spec(state, layer, nb)
    o, s_new = pl.pallas_call(
        _gdn_step_kernel,
        grid=(n // nb,),
        in_specs=[row] * 6 + [pl.BlockSpec((1, DV_C), lambda i: (0, 0)), st, pl.BlockSpec(memory_space=pl.ANY)],
        out_specs=[row, st],
        out_shape=[jax.ShapeDtypeStruct((n, H_C, 1, DV_C), F32), jax.ShapeDtypeStruct(prev.shape, prev.dtype)],
        input_output_aliases={8: 1},
        compiler_params=_cparams("parallel"),
    )(*rows, norm_g.reshape(1, DV_C), state, prev)
    return o.reshape(n, BR_W), s_new


def _diff_lambda(lam_ref, lam_init):
    lv = lam_ref[...]
    s1 = jnp.sum(lv[0:1] * lv[1:2], axis=1, keepdims=True)
    s2 = jnp.sum(lv[2:3] * lv[3:4], axis=1, keepdims=True)
    return jnp.exp(s1) - jnp.exp(s2) + lam_init


def _diff_out_norm(o, g, lam_init):
    return o * lax.rsqrt(jnp.mean(o * o, axis=1, keepdims=True) + 1e-5) * g * (1.0 - lam_init)


def _row_reduce(x, op):
    rows, cols = x.shape
    part = op(x.reshape(REDUCE_WAYS, rows // (REDUCE_WAYS * SUBLANES), SUBLANES, cols), axis=1)
    return op(op(part, axis=0), axis=0, keepdims=True)


def _diff_flash_kernel(q_ref, k_ref, v_ref, lam_ref, g_ref, o_ref, qs_ref, m_ref, l_ref, acc_ref, s_ref, *, lam_init):
    i = pl.program_id(2)
    tq = q_ref.shape[0]
    q = q_ref[...] * (DK_B ** -0.5)
    lane = _iota(q.shape, 1)
    qs_ref[:tq] = jnp.where(lane < DK_B, q, 0.0).astype(BF16)
    qs_ref[tq:] = jnp.where(lane >= DK_B, q, 0.0).astype(BF16)
    m_ref[...] = jnp.full_like(m_ref, NEG_INF)
    l_ref[...] = jnp.zeros_like(l_ref)
    acc_ref[...] = jnp.zeros_like(acc_ref)

    n_grp = 2 * tq // FLASH_COLS
    grp = [slice(g * FLASH_COLS, (g + 1) * FLASH_COLS) for g in range(n_grp)]

    def kv_rows(j):
        return pl.ds(pl.multiple_of(j * tq, tq), tq)

    def scores(j):
        s_ref[j % 2] = _dot_nt(k_ref[kv_rows(j), :].astype(BF16), qs_ref[...])

    def consume(j, diagonal):
        v_t = v_ref[kv_rows(j), :].T.astype(BF16)
        slot = j % 2
        s = [s_ref[slot, :, sl] for sl in grp]
        if diagonal:
            row = _iota(s[0].shape, 0)
            col = _iota(s[0].shape, 1)
            s = [jnp.where(row <= ((col + g * FLASH_COLS) & (tq - 1)), z, NEG_INF) for g, z in enumerate(s)]
        m_old = [m_ref[:, sl] for sl in grp]
        m_new = [jnp.maximum(mo, _row_reduce(z, jnp.max)) for mo, z in zip(m_old, s)]
        p = [jnp.exp(z - mn) for z, mn in zip(s, m_new)]
        pv = [_dot(v_t, z.astype(BF16)) for z in p]
        for sl, mo, mn, z, zv in zip(grp, m_old, m_new, p, pv):
            alpha = jnp.exp(mo - mn)
            l_ref[:, sl] = alpha * l_ref[:, sl] + _row_reduce(z, jnp.sum)
            acc_ref[:, sl] = alpha * acc_ref[:, sl] + zv
            m_ref[:, sl] = mn

    scores(0)

    def body(j, carry):
        consume(j - 1, False)
        scores(j)
        return carry
    lax.fori_loop(1, i + 1, body, 0)
    consume(i, True)
    lam = _diff_lambda(lam_ref, lam_init)
    on = acc_ref[...] / l_ref[...]
    o = (on[:, :tq] - lam * on[:, tq:]).T
    o_ref[...] = _diff_out_norm(o, g_ref[...], lam_init)


def _diff_flash(proj, lam, norm_g, lam_init, tq):
    bsz, t, _ = proj.shape
    kq, kk, kv = (BLK512[n] * (BR_W // LANES) for n in ('b_q', 'b_k', 'b_v'))
    return pl.pallas_call(
        functools.partial(_diff_flash_kernel, lam_init=lam_init),
        grid=(bsz, H_B, t // tq),
        in_specs=[pl.BlockSpec((None, tq, LANES), lambda b, h, i: (b, i, kq + h)),
                  pl.BlockSpec((None, t, LANES), lambda b, h, i: (b, 0, kk + h)),
                  pl.BlockSpec((None, t, LANES), lambda b, h, i: (b, 0, kv + h)),
                  pl.BlockSpec((4, DK_B), lambda b, h, i: (0, 0)),
                  pl.BlockSpec((1, DV_B), lambda b, h, i: (0, 0))],
        out_specs=pl.BlockSpec((None, tq, LANES), lambda b, h, i: (b, i, h)),
        out_shape=jax.ShapeDtypeStruct((bsz, t, BR_W), F32),
        scratch_shapes=[pltpu.VMEM((2 * tq, LANES), BF16), pltpu.VMEM((1, 2 * tq), F32),
                        pltpu.VMEM((1, 2 * tq), F32), pltpu.VMEM((DV_B, 2 * tq), F32),
                        pltpu.VMEM((2, tq, 2 * tq), F32)],
        compiler_params=_cparams("parallel", "parallel", "arbitrary"),
    )(proj, proj, proj, lam, norm_g.reshape(1, DV_B))


def _head_rows(x, n_rep):
    rows = []
    for h in range(x.shape[1] // LANES):
        rows += [x[:, h * LANES:(h + 1) * LANES]] * n_rep
    return jnp.concatenate(rows, axis=0)


def _diff_decode_kernel(pt_ref, q_ref, kn_ref, vn_ref, *rest, lam_init, npg):
    del pt_ref
    k_refs, v_refs = rest[:npg], rest[npg:2 * npg]
    lam_ref, g_ref, o_ref = rest[2 * npg:]
    nrow = 2 * H_B
    q8 = _head_rows(q_ref[...] * (DK_B ** -0.5), 2)
    lane = _iota(q8.shape, 1)
    row = _iota(q8.shape, 0)
    q8 = jnp.where((lane >= DK_B) == ((row & 1) == 1), q8, 0.0)
    q8b = q8.astype(BF16)
    ncol = k_refs[0].shape[0]
    valid = (_iota((nrow, ncol), 1) & (H_B - 1)) == (_iota((nrow, ncol), 0) >> 1)
    s_new = jnp.sum(q8 * _head_rows(kn_ref[...], 2), axis=1, keepdims=True)
    s_pages = [jnp.where(valid, _dot_nt(q8b, k[...].astype(BF16)), NEG_INF) for k in k_refs]
    m = s_new
    for s in s_pages:
        m = jnp.maximum(m, jnp.max(s, axis=1, keepdims=True))
    p_new = jnp.exp(s_new - m)
    l_sum = p_new
    acc = p_new * _head_rows(vn_ref[...], 2)
    for s, v in zip(s_pages, v_refs):
        p = jnp.exp(s - m)
        l_sum = l_sum + jnp.sum(p, axis=1, keepdims=True)
        acc = acc + _dot(p.astype(BF16), v[...].astype(BF16))
    on = acc / l_sum
    lam = _diff_lambda(lam_ref, lam_init)
    for h in range(H_B):
        o = on[2 * h:2 * h + 1] - lam * on[2 * h + 1:2 * h + 2]
        o_ref[:, h * DV_B:(h + 1) * DV_B] = _diff_out_norm(o, g_ref[...], lam_init)


def _diff_decode(proj, cache_k, cache_v, page_table, lam, norm_g, lam_init, layer):
    n = proj.shape[0]
    npg = page_table.shape[1]

    def col(name):
        return pl.BlockSpec((None, 1, BR_W), lambda b, pt: (b, 0, BLK512[name]))

    def page(i):
        return pl.BlockSpec((None, None, PAGE * H_B, DV_B), lambda b, pt: (layer, pt[b * npg + i], 0, 0))
    pages = [page(i) for i in range(npg)]
    grid_spec = pltpu.PrefetchScalarGridSpec(
        num_scalar_prefetch=1,
        grid=(n,),
        in_specs=[col('b_q'), col('b_k'), col('b_v')] + pages + pages
                 + [pl.BlockSpec((4, DK_B), lambda b, pt: (0, 0)), pl.BlockSpec((1, DV_B), lambda b, pt: (0, 0))],
        out_specs=pl.BlockSpec((None, 1, BR_W), lambda b, pt: (b, 0, 0)))
    return pl.pallas_call(
        functools.partial(_diff_decode_kernel, lam_init=lam_init, npg=npg),
        grid_spec=grid_spec,
        out_shape=jax.ShapeDtypeStruct((n, 1, BR_W), F32),
        compiler_params=_cparams("parallel"),
    )(page_table.reshape(-1), proj, proj, proj, *([cache_k] * npg), *([cache_v] * npg), lam,
      norm_g.reshape(1, DV_B))


def _cross_kernel(q_ref, k_ref, v_ref, o_ref):
    hsl = [slice(h * HD_X, (h + 1) * HD_X) for h in range(H_X)]
    s = [_dot_nt(q_ref[:, sl].astype(BF16), k_ref[:, sl].astype(BF16)) * (HD_X ** -0.5) for sl in hsl]
    p = [jnp.exp(z - jnp.max(z, axis=1, keepdims=True)) for z in s]
    o = [_dot(z.astype(BF16), v_ref[:, sl].astype(BF16)) for z, sl in zip(p, hsl)]
    for sl, z, oz in zip(hsl, p, o):
        o_ref[:, sl] = oz / jnp.sum(z, axis=1, keepdims=True)


def _cross_prompt(proj, mem_kv, tq):
    bsz, t, _ = proj.shape
    n_mem = mem_kv.shape[1]
    return pl.pallas_call(
        _cross_kernel,
        grid=(bsz, t // tq),
        in_specs=[pl.BlockSpec((None, tq, BR_W), lambda b, i: (b, i, BLK512['x_q'])),
                  pl.BlockSpec((None, n_mem, BR_W), lambda b, i: (b, 0, 0)),
                  pl.BlockSpec((None, n_mem, BR_W), lambda b, i: (b, 0, 1))],
        out_specs=pl.BlockSpec((None, tq, BR_W), lambda b, i: (b, i, 0)),
        out_shape=jax.ShapeDtypeStruct((bsz, t, BR_W), F32),
        compiler_params=_cparams("parallel", "parallel"),
    )(proj, mem_kv, mem_kv)


def _cross_decode_kernel(q_ref, k_ref, v_ref, o_ref):
    nb = q_ref.shape[0]
    ncol = k_ref.shape[1]
    valid = (_iota((SUBLANES, ncol), 1) & (H_X - 1)) == _iota((SUBLANES, ncol), 0)
    for b in range(nb):
        q8 = jnp.concatenate([_head_rows(q_ref[b], 1), jnp.zeros((SUBLANES - H_X, HD_X), F32)], axis=0)
        s = _dot_nt(q8.astype(BF16), k_ref[b].astype(BF16)) * (HD_X ** -0.5)
        s = jnp.where(valid, s, NEG_INF)
        m = jnp.max(s, axis=1, keepdims=True)
        p = jnp.exp(s - m)
        o = _dot(p.astype(BF16), v_ref[b].astype(BF16)) / jnp.sum(p, axis=1, keepdims=True)
        for h in range(H_X):
            o_ref[b, :, h * HD_X:(h + 1) * HD_X] = o[h:h + 1]


def _cross_decode(proj, cache_k, cache_v, layer, nb):
    n = proj.shape[0]
    rows = cache_k.shape[2]
    kv = pl.BlockSpec((None, nb, rows, HD_X), lambda i: (layer, i, 0, 0))
    return pl.pallas_call(
        _cross_decode_kernel,
        grid=(n // nb,),
        in_specs=[pl.BlockSpec((nb, 1, BR_W), lambda i: (i, 0, BLK512['x_q'])), kv, kv],
        out_specs=pl.BlockSpec((nb, 1, BR_W), lambda i: (i, 0, 0)),
        out_shape=jax.ShapeDtypeStruct((n, 1, BR_W), F32),
        compiler_params=_cparams("parallel"),
    )(proj, cache_k, cache_v)


def _merge_kernel(x_ref, oa_ref, az_ref, ob_ref, bz_ref, oc_ref, ox_ref, xz_ref, mg_ref, wb_ref, wo_ref, o_ref):
    branches = (oa_ref[...] * _silu(az_ref[...]), ob_ref[...] * _silu(bz_ref[...]), oc_ref[...],
                ox_ref[...] * _silu(xz_ref[...]))
    acc = None
    for n, br in enumerate(branches):
        gate = _sigmoid(mg_ref[:, n * D_MODEL:(n + 1) * D_MODEL])
        term = gate * _dot(br.astype(BF16), wb_ref[n])
        acc = term if acc is None else acc + term
    o_ref[...] = x_ref[...] + _dot(acc.astype(BF16), wo_ref[...])


def _merge(x, proj, o_a, o_b, o_c, o_x, w_branch, w_out, tm):
    n = x.shape[0]

    def col(name):
        return pl.BlockSpec((tm, BR_W), lambda i: (i, BLK512[name]))
    br = pl.BlockSpec((tm, BR_W), lambda i: (i, 0))
    return pl.pallas_call(
        _merge_kernel,
        grid=(n // tm,),
        in_specs=[pl.BlockSpec((tm, D_MODEL), lambda i: (i, 0)),
                  br, col('a_z'), br, col('b_z'), br, br, col('x_z'),
                  pl.BlockSpec((tm, MERGE_W), lambda i: (i, 0)),
                  pl.BlockSpec((N_BRANCH, BR_W, D_MODEL), lambda i: (0, 0, 0)),
                  pl.BlockSpec((D_MODEL, D_MODEL), lambda i: (0, 0))],
        out_specs=pl.BlockSpec((tm, D_MODEL), lambda i: (i, 0)),
        out_shape=jax.ShapeDtypeStruct((n, D_MODEL), F32),
        compiler_params=_cparams("parallel"),
    )(x, o_a, proj, o_b, proj, o_c, o_x, proj, proj, w_branch, w_out)


def _kv_layout_kernel(k_ref, v_ref, pk_ref, pv_ref, ok_ref, ov_ref):
    del pk_ref, pv_ref
    for src, dst in ((k_ref, ok_ref), (v_ref, ov_ref)):
        for h in range(H_B):
            dst[:, h, :] = src[:, h * LANES:(h + 1) * LANES]


def _kv_layout(proj, slot, prev, tm):
    bsz, t, _ = proj.shape
    out = pl.BlockSpec((None, None, tm, H_B, LANES), lambda b, i: (slot, b, i, 0, 0))
    return pl.pallas_call(
        _kv_layout_kernel,
        grid=(bsz, t // tm),
        in_specs=[pl.BlockSpec((None, tm, BR_W), lambda b, i: (b, i, BLK512['b_k'])),
                  pl.BlockSpec((None, tm, BR_W), lambda b, i: (b, i, BLK512['b_v'])),
                  pl.BlockSpec(memory_space=pl.ANY), pl.BlockSpec(memory_space=pl.ANY)],
        out_specs=[out, out],
        out_shape=[jax.ShapeDtypeStruct(z.shape, z.dtype) for z in prev],
        input_output_aliases={2: 0, 3: 1},
        compiler_params=_cparams("parallel", "parallel"),
    )(proj, proj, *prev)


def _merge_token_head(cache):
    return cache.reshape(*cache.shape[:-3], cache.shape[-3] * cache.shape[-2], cache.shape[-1])


def _permute_w_in(w_in):
    d = w_in.shape[0]
    parts = [w_in[..., REF_OFF['merge']:REF_OFF['merge'] + MERGE_W]]
    parts += [w_in[..., REF_OFF[n]:REF_OFF[n] + BR_W] for n in W512]
    parts.append(w_in[..., REF_OFF['a_lora']:REF_OFF['a_lora'] + 2 * LORA])
    parts.append(w_in[..., REF_OFF['c_ab']:REF_OFF['c_ab'] + 2 * H_C])
    parts.append(jnp.zeros(w_in.shape[:-1] + (LANES - 2 * H_C,), w_in.dtype))
    del d
    return jnp.concatenate(parts, axis=-1).astype(BF16)


def _layer_params(p, l):
    mu = p['rwkv_mu'][l]
    row = lambda z: z.reshape(1, -1)
    zeros64 = jnp.zeros((LORA, BR_W), F32)
    rwkv = (row(mu[:BR_W]), row(mu[BR_W:2 * BR_W]), row(mu[2 * BR_W:3 * BR_W]), row(mu[3 * BR_W:]),
            row(p['rwkv_w0'][l]), jnp.concatenate([p['rwkv_w2'][l], zeros64], axis=0),
            row(p['rwkv_a0'][l]), jnp.concatenate([zeros64, p['rwkv_a2'][l]], axis=0),
            row(p['rwkv_kk'][l]), row(p['rwkv_ka'][l]))
    wc = p['gdn_conv_w'][l]
    pad = lambda z: jnp.pad(z, (0, LANES - z.shape[0])).reshape(1, LANES)
    gdn = (wc[:, :BR_W], wc[:, BR_W:2 * BR_W], wc[:, 2 * BR_W:], pad(p['gdn_a_log'][l]), pad(p['gdn_dt_bias'][l]))
    return rwkv, gdn


def _prompt_layer(x, mem, p, wts, l, kv, slot):
    bsz, t, d = x.shape
    n = bsz * t
    rwkv_prm, gdn_prm = _layer_params(p, l)
    lam_init = 0.8 - 0.6 * math.exp(-0.3 * l)
    mem_kv = _norm_matmul(mem.reshape(-1, d), p['norm_mem_g'][l], wts['w_mem'][l], 256, 512)
    mem_kv = mem_kv.reshape(bsz, -1, 2 * BR_W)
    proj = _norm_matmul(x.reshape(n, d), p['norm_g'][l], wts['w_in'][l], min(PROJ_TM, n), PROJ_TN)
    proj3 = proj.reshape(bsz, t, PROJ_W)
    row = lambda z: z.reshape(1, -1)
    o_a, s_rwkv = _rwkv_chunk(proj3, rwkv_prm, row(p['rwkv_ln_g'][l]), row(p['rwkv_ln_b'][l]), row(p['rwkv_rk'][l]))
    o_b = _diff_flash(proj3, p['diff_lam'][l], p['diff_norm_g'][l], lam_init, min(FLASH_TQ, t // 2))
    o_c, s_gdn = _gdn_chunk(proj3, gdn_prm, p['gdn_norm_g'][l])
    o_x = _cross_prompt(proj3, mem_kv, min(512, t))
    flat = lambda z: z.reshape(n, BR_W)
    x_new = _merge(x.reshape(n, d), proj, flat(o_a), flat(o_b), flat(o_c), flat(o_x), wts['w_branch'][l],
                   wts['w_out'][l], min(MERGE_TM, n))

    def cols(name, width=BR_W):
        off = BLK512[name] * BR_W
        return proj3[:, :, off:off + width]
    kv = _kv_layout(proj3, slot, kv, min(KV_TM, t))
    last = proj3[:, t - 1:, :]
    a_off = BLK512['a_r'] * BR_W
    shift = jnp.concatenate([last[:, :, a_off:a_off + 3 * BR_W],
                             last[:, :, LORA_BLK * LANES:LORA_BLK * LANES + 2 * LORA]], axis=-1)
    c_off = BLK512['c_q'] * BR_W
    conv = proj3[:, t - (CONV_W - 1):, c_off:c_off + 3 * BR_W]
    mk = mem_kv[:, :, :BR_W].reshape(bsz, -1, H_X, HD_X)
    mv = mem_kv[:, :, BR_W:].reshape(bsz, -1, H_X, HD_X)
    return x_new.reshape(bsz, t, d), (s_rwkv, shift, s_gdn, conv, mk, mv), kv


def _sample_layer(x, caches, p, wts, l, prev):
    n, _, d = x.shape
    (cache_k, cache_v, page_table, cache_mk, cache_mv, st_rwkv, st_shift, st_gdn, st_conv) = caches
    rwkv_prm, gdn_prm = _layer_params(p, l)
    lam_init = 0.8 - 0.6 * math.exp(-0.3 * l)
    proj = _norm_matmul(x.reshape(n, d), p['norm_g'][l], wts['w_in'][l], n, PROJ_TN)
    proj3 = proj.reshape(n, 1, PROJ_W)
    sh = st_shift[l].reshape(n, -1)
    shift_in = (sh[:, :BR_W], sh[:, BR_W:2 * BR_W], sh[:, 2 * BR_W:3 * BR_W], sh[:, 3 * BR_W:])
    rows = _rwkv_prep_sample(proj, shift_in, rwkv_prm)
    o_a, s_rwkv = _rwkv_step(rows, p['rwkv_ln_g'][l], p['rwkv_ln_b'][l], p['rwkv_rk'][l], st_rwkv, l, 8, prev[0])
    o_b = _diff_decode(proj3, cache_k, cache_v, page_table, p['diff_lam'][l], p['diff_norm_g'][l], lam_init, l)
    buf = jnp.moveaxis(st_conv[l], 1, 0)
    bufs = (buf[:, :, :BR_W], buf[:, :, BR_W:2 * BR_W], buf[:, :, 2 * BR_W:])
    gq, gk, gv, eg, beta = _gdn_prep_sample(proj, bufs, gdn_prm)
    z_off = BLK512['c_z'] * BR_W
    o_c, s_gdn = _gdn_step(gq, gk, gv, eg, beta, proj[:, z_off:z_off + BR_W], p['gdn_norm_g'][l], st_gdn, l, 4,
                           prev[1])
    o_x = _cross_decode(proj3, cache_mk, cache_mv, l, 4)
    x_new = _merge(x.reshape(n, d), proj, o_a, o_b.reshape(n, BR_W), o_c, o_x.reshape(n, BR_W), wts['w_branch'][l],
                   wts['w_out'][l], n)

    def cols(name, width=BR_W):
        off = BLK512[name] * BR_W
        return proj[:, off:off + width]
    k_new = cols('b_k').reshape(n, 1, H_B, 2 * DK_B)
    v_new = cols('b_v').reshape(n, 1, H_B, DV_B)
    shift = jnp.concatenate([cols('a_r', 3 * BR_W), proj[:, LORA_BLK * LANES:LORA_BLK * LANES + 2 * LORA]],
                            axis=-1).reshape(n, 1, -1)
    conv = jnp.concatenate([st_conv[l][:, 1:], cols('c_q', 3 * BR_W).reshape(n, 1, -1)], axis=1)
    return x_new.reshape(n, 1, d), (k_new, v_new, s_rwkv, shift, s_gdn, conv)


def kernel(x_prompt, x_sample, mem_prompt, cache_diff_k, cache_diff_v, page_table, cache_mem_k, cache_mem_v,
           state_rwkv, state_rwkv_shift, state_gdn, state_gdn_conv, norm_g, w_in, rwkv_mu, rwkv_w0, rwkv_w2,
           rwkv_a0, rwkv_a2, rwkv_kk, rwkv_ka, rwkv_rk, rwkv_ln_g, rwkv_ln_b, diff_lam, diff_norm_g, gdn_conv_w,
           gdn_a_log, gdn_dt_bias, gdn_norm_g, norm_mem_g, w_mem_kv, w_branch, w_out, final_norm_g):
    p = {'norm_g': norm_g, 'rwkv_mu': rwkv_mu, 'rwkv_w0': rwkv_w0, 'rwkv_w2': rwkv_w2, 'rwkv_a0': rwkv_a0,
         'rwkv_a2': rwkv_a2, 'rwkv_kk': rwkv_kk, 'rwkv_ka': rwkv_ka, 'rwkv_rk': rwkv_rk, 'rwkv_ln_g': rwkv_ln_g,
         'rwkv_ln_b': rwkv_ln_b, 'diff_lam': diff_lam, 'diff_norm_g': diff_norm_g, 'gdn_conv_w': gdn_conv_w,
         'gdn_a_log': gdn_a_log, 'gdn_dt_bias': gdn_dt_bias, 'gdn_norm_g': gdn_norm_g, 'norm_mem_g': norm_mem_g}
    depth = w_in.shape[0]
    wts = {'w_in': _permute_w_in(w_in), 'w_mem': w_mem_kv.astype(BF16), 'w_branch': w_branch.astype(BF16),
           'w_out': w_out.astype(BF16)}
    n_s = x_sample.shape[0]
    caches = (_merge_token_head(cache_diff_k), _merge_token_head(cache_diff_v), page_table,
              _merge_token_head(cache_mem_k), _merge_token_head(cache_mem_v), state_rwkv, state_rwkv_shift, state_gdn,
              state_gdn_conv)
    xp, xs = x_prompt, x_sample
    outs_p, outs_s = [], []
    bsz_p, t_p = x_prompt.shape[:2]
    kv = tuple(jnp.zeros((depth, bsz_p, t_p, H_B, LANES), F32) for _ in range(2))
    stacked = (jnp.zeros_like(state_rwkv), jnp.zeros_like(state_gdn))
    for l in range(depth):
        xp, st_p, kv = _prompt_layer(xp, mem_prompt, p, wts, l, kv, l)
        xs, st_s = _sample_layer(xs, caches, p, wts, l, stacked)
        stacked = (st_s[2], st_s[4])
        outs_p.append(st_p)
        outs_s.append(st_s)
    bsz, t, d = xp.shape
    y_prompt = _final_norm(xp.reshape(bsz * t, d), final_norm_g, min(512, bsz * t)).reshape(bsz, t, d)
    y_sample = _final_norm(xs.reshape(n_s, d), final_norm_g, n_s).reshape(n_s, 1, d)
    diff_k_prompt, diff_v_prompt = kv
    sp = lambda i: jnp.stack([o[i] for o in outs_p])
    ss = lambda i: jnp.stack([o[i] for o in outs_s])
    return (y_prompt, y_sample, diff_k_prompt, diff_v_prompt, ss(0), ss(1), sp(4), sp(5), sp(0), stacked[0], sp(1),
            ss(3), sp(2), stacked[1], sp(3), ss(5))
```

```python
import functools
import math

import jax
import jax.numpy as jnp
from jax import lax
from jax.experimental import pallas as pl
from jax.experimental.pallas import tpu as pltpu

F32 = jnp.float32
BF16 = jnp.bfloat16
HI = lax.Precision.HIGHEST

D_MODEL = 1024
BR_W = D_MODEL // 2
N_BRANCH = 4
N_A = 64
H_A = BR_W // N_A
LORA = 64
H_B = 4
DK_B = 64
DV_B = 128
H_C = 4
DK_C = 128
DV_C = 128
CONV_W = 4
CHUNK = 64
H_X = 4
HD_X = BR_W // H_X
PAGE = 128
RWKV_GN_EPS = 64e-5
NORM_EPS = 1e-6
NEG_INF = -1e30
LOG2_E = math.log2(math.e)
LANES = 128
SUBLANES = 8
VMEM_LIMIT = 48 * 1024 * 1024
PROJ_TM = 1024
PROJ_TN = 1280
MERGE_TM = 256
REDUCE_WAYS = 8
KV_TM = 512
RWKV_NCH = 2
GDN_NCH = 2
FLASH_TQ = 512
FLASH_COLS = 256

W512 = ('a_r', 'a_k', 'a_v', 'a_z', 'b_q', 'b_k', 'b_v', 'b_z', 'c_q', 'c_k', 'c_v', 'c_z', 'x_q', 'x_z')
MERGE_W = N_BRANCH * D_MODEL
BLK512 = {name: MERGE_W // BR_W + i for i, name in enumerate(W512)}
LORA_BLK = (MERGE_W + len(W512) * BR_W) // LANES
CAB_BLK = LORA_BLK + 1
PROJ_W = (CAB_BLK + 1) * LANES
REF_OFF = {'a_r': 0, 'a_k': 512, 'a_v': 1024, 'a_lora': 1536, 'a_z': 1664, 'b_q': 2176, 'b_k': 2688, 'b_v': 3200,
           'b_z': 3712, 'c_q': 4224, 'c_k': 4736, 'c_v': 5248, 'c_ab': 5760, 'c_z': 5768, 'x_q': 6280, 'x_z': 6792,
           'merge': 7304}


def _cparams(*sem):
    return pltpu.CompilerParams(dimension_semantics=sem, vmem_limit_bytes=VMEM_LIMIT)


def _sigmoid(x):
    return 1.0 / (1.0 + jnp.exp(-x))


def _silu(x):
    return x * _sigmoid(x)


def _softplus(x):
    return jnp.maximum(x, 0.0) + jnp.log(1.0 + jnp.exp(-jnp.abs(x)))


def _dot(a, b, precision=None):
    return jnp.dot(a, b, preferred_element_type=F32, precision=precision)


def _dot_nt(a, b, precision=None):
    return lax.dot_general(a, b, (((1,), (1,)), ((), ())), preferred_element_type=F32, precision=precision)


def _iota(shape, axis):
    return lax.broadcasted_iota(jnp.int32, shape, axis)


def _split(x):
    hi = x.astype(BF16)
    return hi, (x - hi.astype(F32)).astype(BF16)


def _seg_allsum64(xs):
    same = (_iota((LANES, LANES), 0) // N_A) == (_iota((LANES, LANES), 1) // N_A)
    ones = jnp.where(same, 1.0, 0.0).astype(BF16)
    ones2 = jnp.concatenate([ones, ones], axis=0)
    return [_dot(jnp.concatenate(_split(x), axis=1), ones2) for x in xs]


def _lhs3(x):
    hi, lo = _split(x)
    return jnp.concatenate([hi, hi, lo], axis=1)


def _rhs3(x):
    hi, lo = _split(x)
    return jnp.concatenate([hi, lo, hi], axis=0)


def _bdot(a, b):
    return _dot(a.astype(BF16), b.astype(BF16))


def _bdot_nt(a, b):
    return _dot_nt(a.astype(BF16), b.astype(BF16))


def _tri_inverse(ls, split):
    n = ls[0].shape[0]
    eye = (_iota((n, n), 0) == _iota((n, n), 1)).astype(F32)
    lhs, rhs = (_lhs3, _rhs3) if split else ((lambda z: z.astype(BF16)),) * 2
    xs = [eye + l for l in ls]
    ps = [_dot(lhs(l), rhs(l)) for l in ls]
    for _ in range(4):
        xps = [_dot(lhs(jnp.concatenate([x, p], axis=0)), rhs(p)) for x, p in zip(xs, ps)]
        xs = [x + xp[:n] for x, xp in zip(xs, xps)]
        ps = [xp[n:] for xp in xps]
    return [x + _dot(lhs(x), rhs(p)) for x, p in zip(xs, ps)]


def _norm_matmul_kernel(x_ref, g_ref, w_ref, o_ref, h_ref):
    @pl.when(pl.program_id(1) == 0)
    def _():
        x = x_ref[...]
        y = x * lax.rsqrt(jnp.mean(x * x, axis=-1, keepdims=True) + NORM_EPS)
        h_ref[...] = (y * g_ref[...]).astype(BF16)
    o_ref[...] = _dot(h_ref[...], w_ref[...])


def _norm_matmul(x, g, w, tm, tn):
    n, d = x.shape
    c = w.shape[1]
    return pl.pallas_call(
        _norm_matmul_kernel,
        grid=(n // tm, c // tn),
        in_specs=[pl.BlockSpec((tm, d), lambda i, j: (i, 0)),
                  pl.BlockSpec((1, d), lambda i, j: (0, 0)),
                  pl.BlockSpec((d, tn), lambda i, j: (0, j))],
        out_specs=pl.BlockSpec((tm, tn), lambda i, j: (i, j)),
        out_shape=jax.ShapeDtypeStruct((n, c), F32),
        scratch_shapes=[pltpu.VMEM((tm, d), BF16)],
        compiler_params=_cparams("parallel", "arbitrary"),
    )(x, g.reshape(1, d), w)


def _final_norm_kernel(x_ref, g_ref, o_ref):
    x = x_ref[...]
    o_ref[...] = x * lax.rsqrt(jnp.mean(x * x, axis=-1, keepdims=True) + NORM_EPS) * g_ref[...]


def _final_norm(x, g, tm):
    n, d = x.shape
    return pl.pallas_call(
        _final_norm_kernel,
        grid=(n // tm,),
        in_specs=[pl.BlockSpec((tm, d), lambda i: (i, 0)), pl.BlockSpec((1, d), lambda i: (0, 0))],
        out_specs=pl.BlockSpec((tm, d), lambda i: (i, 0)),
        out_shape=jax.ShapeDtypeStruct((n, d), F32),
        compiler_params=_cparams("parallel"),
    )(x, g.reshape(1, d))


def _rwkv_prep_math(cur, prev, prm):
    mu_r, mu_k, mu_v, mu_l, w0, w2p, a0, a2p, kkp, kap = prm
    r = cur[0] + (prev[0] - cur[0]) * mu_r
    k = cur[1] + (prev[1] - cur[1]) * mu_k
    v = cur[2] + (prev[2] - cur[2]) * mu_v
    lo = cur[3] + (prev[3] - cur[3]) * mu_l
    w_log = -_softplus(-(w0 + _dot(jnp.tanh(lo), w2p))) - 0.5
    log_decay = -jnp.exp(w_log)
    a = _sigmoid(a0 + _dot(lo, a2p))
    kk = k * kkp
    kk2 = kk * kk
    ssq = _seg_allsum64([kk2[:, i * LANES:(i + 1) * LANES] for i in range(BR_W // LANES)])
    kk = kk * lax.rsqrt(jnp.concatenate(ssq, axis=1) + 1e-6)
    k = k * (1.0 + (a - 1.0) * kap)
    return r, log_decay, k, v, -kk, kk * a


def _rwkv_prep_sample_kernel(pr, pk, pv, plo, qr, qk, qv, qlo, mu_r, mu_k, mu_v, mu_l, w0, w2p, a0, a2p, kkp, kap,
                             o_r, o_w, o_k, o_v, o_a, o_b):
    cur = (pr[...], pk[...], pv[...], plo[...])
    prev = (qr[...], qk[...], qv[...], qlo[...])
    prm = tuple(z[...] for z in (mu_r, mu_k, mu_v, mu_l, w0, w2p, a0, a2p, kkp, kap))
    outs = _rwkv_prep_math(cur, prev, prm)
    for o, val in zip((o_r, o_w, o_k, o_v, o_a, o_b), outs):
        o[...] = val


def _full_spec(a):
    nd = a.ndim
    return pl.BlockSpec(a.shape, lambda *_: (0,) * nd)


def _rwkv_prep_sample(proj, shift, prm):
    n = proj.shape[0]

    def col(blk, w):
        return pl.BlockSpec((n, w), lambda i: (0, blk))
    out = pl.BlockSpec((n, BR_W), lambda i: (0, 0))
    return pl.pallas_call(
        _rwkv_prep_sample_kernel,
        grid=(1,),
        in_specs=[col(BLK512['a_r'], BR_W), col(BLK512['a_k'], BR_W), col(BLK512['a_v'], BR_W), col(LORA_BLK, LANES)]
                 + [_full_spec(s) for s in shift] + [_full_spec(p) for p in prm],
        out_specs=[out] * 6,
        out_shape=[jax.ShapeDtypeStruct((n, BR_W), F32)] * 6,
        compiler_params=_cparams("arbitrary"),
    )(proj, proj, proj, proj, *shift, *prm)


def _stack_pair(x):
    lane = _iota(x.shape, 1)
    return jnp.concatenate([jnp.where(lane < N_A, x, 0.0), jnp.where(lane >= N_A, x, 0.0)], axis=0)


def _rwkv_group_norm_bonus(ys, rs, ks, vs, gs, bs, rks):
    n = ys[0].shape[0]
    sums = _seg_allsum64([jnp.concatenate([y, r * k * rk], axis=0) for y, r, k, rk in zip(ys, rs, ks, rks)])
    ds = [y - s[:n] * (1.0 / N_A) for y, s in zip(ys, sums)]
    vars_ = _seg_allsum64([d * d for d in ds])
    return [d * lax.rsqrt(var * (1.0 / N_A) + RWKV_GN_EPS) * g + b + s[n:] * v
            for d, var, g, b, s, v in zip(ds, vars_, gs, bs, sums, vs)]


def _rwkv_chunk_kernel(pr, pk, pv, plo, mu_r, mu_k, mu_v, mu_l, w0, w2p, a0, a2p, kkp, kap, g_ref, bb_ref, rk_ref,
                       o_ref, s_ref, st_ref, c_r, c_k, c_v, c_l):
    ci = pl.program_id(1)
    nc = pl.num_programs(1)
    c = CHUNK
    carries = (c_r, c_k, c_v, c_l)

    @pl.when(ci == 0)
    def _():
        st_ref[...] = jnp.zeros_like(st_ref)
        for cr in carries:
            cr[...] = jnp.zeros_like(cr)
    cur = (pr[...], pk[...], pv[...], plo[...])
    rows = cur[0].shape[0]
    prev = [jnp.where(_iota(z.shape, 0) == 0, cr[...], pltpu.roll(z, 1, 0)) for z, cr in zip(cur, carries)]
    for z, cr in zip(cur, carries):
        cr[...] = z[rows - 1:rows, :]
    prm = tuple(z[...] for z in (mu_r, mu_k, mu_v, mu_l, w0, w2p, a0, a2p, kkp, kap))
    r, lw, k, v, a_in, b_in = _rwkv_prep_math(cur, prev, prm)
    ti = _iota((rows, rows), 0)
    tj = _iota((rows, rows), 1)
    tri = ((ti >= tj) & ((ti // c) == (tj // c))).astype(F32)
    cum = _dot(tri, lw, HI)
    p_incl = jnp.exp(cum)
    p_inv = jnp.exp(-cum)
    p_prev = jnp.exp(cum - lw)
    at = a_in * p_prev
    bt = b_in * p_inv
    kt = k * p_inv
    rt = r * p_incl
    n2 = 2 * c
    ri = _iota((n2, n2), 0)
    cj = _iota((n2, n2), 1)
    same = (ri >= c) == (cj >= c)
    strict = same & ((ri & (c - 1)) > (cj & (c - 1)))
    incl = same & ((ri & (c - 1)) >= (cj & (c - 1)))
    eye = (ri == cj).astype(F32)
    n_ch = rows // c
    n_pair = H_A // 2
    units = [(ch, p) for ch in range(n_ch) for p in range(n_pair)]

    def rsl(ch):
        return slice(ch * c, (ch + 1) * c)

    def lsl(p):
        return slice(p * LANES, (p + 1) * LANES)
    stk = {name: [_stack_pair(z[rsl(ch), lsl(p)]) for ch, p in units]
           for name, z in (('a', at), ('b', bt), ('k', kt), ('r', rt), ('v', v))}
    prods = [_dot_nt(jnp.concatenate([a_s, r_s], axis=0).astype(BF16),
                     jnp.concatenate([b_s, k_s], axis=0).astype(BF16))
             for a_s, r_s, b_s, k_s in zip(stk['a'], stk['r'], stk['b'], stk['k'])]
    l_ab = [jnp.where(strict, pr[:n2, :n2], 0.0) for pr in prods]
    l_ak = [jnp.where(strict, pr[:n2, n2:], 0.0).astype(BF16) for pr in prods]
    m_r = [jnp.concatenate([jnp.where(incl, pr[n2:, :n2], 0.0), jnp.where(incl, pr[n2:, n2:], 0.0)],
                           axis=1).astype(BF16) for pr in prods]
    v_b = [z.astype(BF16) for z in stk['v']]
    lv = [_dot(l, z) for l, z in zip(l_ak, v_b)]
    t_inv = _tri_inverse(l_ab, split=False)
    wu = [_bdot(t, jnp.concatenate([a_s, z], axis=1)) for t, a_s, z in zip(t_inv, stk['a'], lv)]
    p_last = [p_incl[(ch + 1) * c - 1:(ch + 1) * c, lsl(p)] for ch, p in units]
    bk_t = [jnp.concatenate([(b_s * pl_).T, (k_s * pl_).T], axis=1).astype(BF16)
            for b_s, k_s, pl_ in zip(stk['b'], stk['k'], p_last)]
    p_col = [jnp.sum(eye * pl_, axis=1, keepdims=True) for pl_ in p_last]
    r_b = [z.astype(BF16) for z in stk['r']]
    s_t = [st_ref[p] for p in range(n_pair)]
    for ch in range(n_ch):
        ix = [ch * n_pair + p for p in range(n_pair)]
        s_b = [s.astype(BF16) for s in s_t]
        u_b = [(_dot(wu[i][:, :LANES].astype(BF16), s) + wu[i][:, LANES:]).astype(BF16) for i, s in zip(ix, s_b)]
        y_s = [_dot(jnp.concatenate([r_b[i], m_r[i]], axis=1), jnp.concatenate([s, u, v_b[i]], axis=0))
               for i, s, u in zip(ix, s_b, u_b)]
        s_t = [s * p_col[i] + _dot(bk_t[i], jnp.concatenate([u, v_b[i]], axis=0)) for i, s, u in zip(ix, s_t, u_b)]
        rs = rsl(ch)
        pairs = [lsl(p) for p in range(n_pair)]
        outs = _rwkv_group_norm_bonus([z[:c] + z[c:] for z in y_s], [r[rs, sl] for sl in pairs],
                                      [k[rs, sl] for sl in pairs], [v[rs, sl] for sl in pairs],
                                      [g_ref[:, sl] for sl in pairs], [bb_ref[:, sl] for sl in pairs],
                                      [rk_ref[:, sl] for sl in pairs])
        for sl, out in zip(pairs, outs):
            o_ref[rs, sl] = out
    for p in range(n_pair):
        st_ref[p] = s_t[p]

    @pl.when(ci == nc - 1)
    def _():
        for p in range(H_A // 2):
            s_vk = st_ref[p].T
            s_ref[2 * p] = s_vk[:N_A, :N_A]
            s_ref[2 * p + 1] = s_vk[N_A:, N_A:]


def _rwkv_chunk(proj, prm, ln_g, ln_b, rk):
    bsz, t, _ = proj.shape
    rows = min(RWKV_NCH * CHUNK, t)

    def col(blk, w):
        return pl.BlockSpec((None, rows, w), lambda bi, ci: (bi, ci, blk))
    par = pl.BlockSpec((1, BR_W), lambda bi, ci: (0, 0))
    return pl.pallas_call(
        _rwkv_chunk_kernel,
        grid=(bsz, t // rows),
        in_specs=[col(BLK512['a_r'], BR_W), col(BLK512['a_k'], BR_W), col(BLK512['a_v'], BR_W), col(LORA_BLK, LANES)]
                 + [_full_spec(z) for z in prm] + [par] * 3,
        out_specs=[pl.BlockSpec((None, rows, BR_W), lambda bi, ci: (bi, ci, 0)),
                   pl.BlockSpec((None, H_A, N_A, N_A), lambda bi, ci: (bi, 0, 0, 0))],
        out_shape=[jax.ShapeDtypeStruct((bsz, t, BR_W), F32), jax.ShapeDtypeStruct((bsz, H_A, N_A, N_A), F32)],
        scratch_shapes=[pltpu.VMEM((H_A // 2, LANES, LANES), F32)] + [pltpu.VMEM((1, BR_W), F32)] * 3
                       + [pltpu.VMEM((1, LANES), F32)],
        compiler_params=_cparams("parallel", "arbitrary"),
    )(proj, proj, proj, proj, *prm, ln_g, ln_b, rk)


def _rwkv_step_kernel(r_ref, w_ref, k_ref, v_ref, a_ref, b_ref, g_ref, bb_ref, rk_ref, s_ref, prev_ref, o_ref, so_ref):
    del prev_ref
    nb = s_ref.shape[0]
    eye = (_iota((N_A, N_A), 0) == _iota((N_A, N_A), 1)).astype(F32)

    def body(bi, carry):
        for h in range(H_A):
            s = s_ref[bi, h]
            r, k, v = r_ref[bi, h], k_ref[bi, h], v_ref[bi, h]
            sa = jnp.sum(s * a_ref[bi, h], axis=1, keepdims=True)
            v_col = jnp.sum(eye * v, axis=1, keepdims=True)
            s2 = s * jnp.exp(w_ref[bi, h]) + sa * b_ref[bi, h] + v_col * k
            y_col = jnp.sum(s2 * r, axis=1, keepdims=True)
            y = jnp.sum(eye * y_col, axis=0, keepdims=True)
            mu = jnp.mean(y, axis=1, keepdims=True)
            d = y - mu
            var = jnp.mean(d * d, axis=1, keepdims=True)
            yn = d * lax.rsqrt(var + RWKV_GN_EPS) * g_ref[h] + bb_ref[h]
            o_ref[bi, h] = yn + jnp.sum(r * k * rk_ref[h], axis=1, keepdims=True) * v
            so_ref[bi, h] = s2
        return carry
    lax.fori_loop(0, nb, body, 0)


def _layer_slot_spec(state, layer, nb):
    return pl.BlockSpec((None, nb) + state.shape[2:], lambda i: (layer, i) + (0,) * (state.ndim - 2))


def _rwkv_step(rows, ln_g, ln_b, rk, state, layer, nb, prev):
    n = rows[0].shape[0]
    rows = [z.reshape(n, H_A, 1, N_A) for z in rows]
    prm = [z.reshape(H_A, 1, N_A) for z in (ln_g, ln_b, rk)]
    row = pl.BlockSpec((nb, H_A, 1, N_A), lambda i: (i, 0, 0, 0))
    par = pl.BlockSpec((H_A, 1, N_A), lambda i: (0, 0, 0))
    st = _layer_slot_spec(state, layer, nb)
    o, s_new = pl.pallas_call(
        _rwkv_step_kernel,
        grid=(n // nb,),
        in_specs=[row] * 6 + [par] * 3 + [st, pl.BlockSpec(memory_space=pl.ANY)],
        out_specs=[row, st],
        out_shape=[jax.ShapeDtypeStruct((n, H_A, 1, N_A), F32), jax.ShapeDtypeStruct(prev.shape, prev.dtype)],
        input_output_aliases={10: 1},
        compiler_params=_cparams("parallel"),
    )(*rows, *prm, state, prev)
    return o.reshape(n, BR_W), s_new


def _gdn_norm_qk(q, k):
    qs, ks = [], []
    for h in range(H_C):
        sl = slice(h * DK_C, (h + 1) * DK_C)
        qh, kh = q[:, sl], k[:, sl]
        qs.append(qh * lax.rsqrt(jnp.sum(qh * qh, axis=1, keepdims=True) + 1e-6) * (DK_C ** -0.5))
        ks.append(kh * lax.rsqrt(jnp.sum(kh * kh, axis=1, keepdims=True) + 1e-6))
    return jnp.concatenate(qs, axis=1), jnp.concatenate(ks, axis=1)


def _gdn_gate(ab, alog, dtb):
    lane = _iota(ab.shape, 1)
    return jnp.where(lane < H_C, -jnp.exp(alog) * _softplus(ab + dtb), _sigmoid(ab))


def _gdn_causal_conv(x, w_ref, tail_ref):
    tm = x.shape[0]
    tail = tail_ref[...]
    acc = x * w_ref[CONV_W - 1:CONV_W, :]
    row8 = _iota(tail.shape, 0)
    for s in range(1, CONV_W):
        rolled = pltpu.roll(x, s, 0)
        head = jnp.where(row8 < s, pltpu.roll(tail, s, 0), rolled[:SUBLANES])
        xs = jnp.concatenate([head, rolled[SUBLANES:]], axis=0)
        acc = acc + xs * w_ref[CONV_W - 1 - s:CONV_W - s, :]
    tail_ref[...] = x[tm - SUBLANES:, :]
    return _silu(acc)


def _gdn_prep_sample_kernel(xq, xk, xv, xab, bq, bk, bv, wq, wk, wv, alog, dtb, o_q, o_k, o_v, o_eg, o_beta):
    outs = []
    for x_ref, b_ref, w_ref in zip((xq, xk, xv), (bq, bk, bv), (wq, wk, wv)):
        acc = x_ref[...] * w_ref[CONV_W - 1:CONV_W, :]
        for i in range(CONV_W - 1):
            acc = acc + b_ref[i] * w_ref[i:i + 1, :]
        outs.append(_silu(acc))
    q, k = _gdn_norm_qk(outs[0], outs[1])
    o_q[...] = q
    o_k[...] = k
    o_v[...] = outs[2]
    gb = _gdn_gate(xab[...], alog[...], dtb[...])
    n = gb.shape[0]
    eg = jnp.exp(gb)
    o_eg[...] = jnp.concatenate([jnp.broadcast_to(eg[:, h:h + 1], (n, DV_C)) for h in range(H_C)], axis=1)
    o_beta[...] = jnp.concatenate([jnp.broadcast_to(gb[:, H_C + h:H_C + h + 1], (n, DV_C)) for h in range(H_C)],
                                  axis=1)


def _gdn_prep_sample(proj, bufs, prm):
    n = proj.shape[0]

    def col(blk, w):
        return pl.BlockSpec((n, w), lambda i: (0, blk))
    out = pl.BlockSpec((n, BR_W), lambda i: (0, 0))
    return pl.pallas_call(
        _gdn_prep_sample_kernel,
        grid=(1,),
        in_specs=[col(BLK512['c_q'], BR_W), col(BLK512['c_k'], BR_W), col(BLK512['c_v'], BR_W), col(CAB_BLK, LANES)]
                 + [_full_spec(z) for z in bufs] + [_full_spec(p) for p in prm],
        out_specs=[out] * 5,
        out_shape=[jax.ShapeDtypeStruct((n, BR_W), F32)] * 5,
        compiler_params=_cparams("arbitrary"),
    )(proj, proj, proj, proj, *bufs, *prm)


def _gdn_out_norm(o, g, z):
    return o * lax.rsqrt(jnp.mean(o * o, axis=1, keepdims=True) + NORM_EPS) * g * _silu(z)


def _gdn_chunk_kernel(xq, xk, xv, xab, z_ref, wq, wk, wv, alog, dtb, g_ref, o_ref, s_ref, st_ref, t_q, t_k, t_v):
    ci = pl.program_id(1)
    nc = pl.num_programs(1)
    c = CHUNK
    rows = xab.shape[0]
    n_ch = rows // c

    @pl.when(ci == 0)
    def _():
        st_ref[...] = jnp.zeros_like(st_ref)
        for tl in (t_q, t_k, t_v):
            tl[...] = jnp.zeros_like(tl)
    q_all, k_all = _gdn_norm_qk(_gdn_causal_conv(xq[...], wq, t_q), _gdn_causal_conv(xk[...], wk, t_k))
    v_all = _gdn_causal_conv(xv[...], wv, t_v)
    gb = _gdn_gate(xab[...], alog[...], dtb[...])
    ri = _iota((rows, rows), 0)
    cj = _iota((rows, rows), 1)
    same = (ri // c) == (cj // c)
    incl = same & (ri >= cj)
    strict = same & (ri > cj)
    cum = _dot(incl.astype(F32), gb, HI)
    cum_t = cum.T
    heads = range(H_C)
    hsl = [slice(h * DK_C, (h + 1) * DK_C) for h in heads]
    gc_col = [cum[:, h:h + 1] for h in heads]
    beta = [gb[:, H_C + h:H_C + h + 1] for h in heads]
    decay = [jnp.where(incl, jnp.exp(jnp.where(incl, gc_col[h] - cum_t[h:h + 1, :], 0.0)), 0.0) for h in heads]
    q = [q_all[:, sl] for sl in hsl]
    k = [k_all[:, sl] for sl in hsl]
    kb = [k[h] * beta[h] for h in heads]
    prod = [_dot_nt(jnp.concatenate([kb[h], q[h]], axis=0).astype(BF16), k[h].astype(BF16)) for h in heads]
    t_inv = _tri_inverse([-jnp.where(strict, prod[h][:rows] * decay[h], 0.0) for h in heads], split=False)
    qk = [jnp.where(incl, prod[h][rows:] * decay[h], 0.0) for h in heads]
    uw = [_bdot(t_inv[h], jnp.concatenate([v_all[:, hsl[h]] * beta[h], kb[h] * jnp.exp(gc_col[h])], axis=1))
          for h in heads]
    qg = [q[h] * jnp.exp(gc_col[h]) for h in heads]
    s = [st_ref[h] for h in heads]
    for ch in range(n_ch):
        rs = slice(ch * c, (ch + 1) * c)
        g_last = [cum[(ch + 1) * c - 1:(ch + 1) * c, h:h + 1] for h in heads]
        s_b = [z.astype(BF16) for z in s]
        v_nb = [(uw[h][rs, :DV_C] - _dot(uw[h][rs, DV_C:].astype(BF16), s_b[h])).astype(BF16) for h in heads]
        zero = jnp.zeros_like(v_nb[0])
        o = [_dot(jnp.concatenate([qg[h][rs], qk[h][rs]], axis=1).astype(BF16),
                  jnp.concatenate([s_b[h]] + [v_nb[h] if i == ch else zero for i in range(n_ch)], axis=0))
             for h in heads]
        s = [s[h] * jnp.exp(g_last[h]) + _dot((k[h][rs] * jnp.exp(g_last[h] - gc_col[h][rs])).T.astype(BF16), v_nb[h])
             for h in heads]
        for h in heads:
            o_ref[rs, hsl[h]] = _gdn_out_norm(o[h], g_ref[...], z_ref[rs, hsl[h]])
    for h in heads:
        st_ref[h] = s[h]

    @pl.when(ci == nc - 1)
    def _():
        s_ref[...] = st_ref[...]


def _gdn_chunk(proj, prm, norm_g):
    bsz, t, _ = proj.shape
    rows = min(GDN_NCH * CHUNK, t)

    def col(blk, w):
        return pl.BlockSpec((None, rows, w), lambda bi, ci: (bi, ci, blk))
    return pl.pallas_call(
        _gdn_chunk_kernel,
        grid=(bsz, t // rows),
        in_specs=[col(BLK512['c_q'], BR_W), col(BLK512['c_k'], BR_W), col(BLK512['c_v'], BR_W), col(CAB_BLK, LANES),
                  col(BLK512['c_z'], BR_W)] + [_full_spec(z) for z in prm]
                 + [pl.BlockSpec((1, DV_C), lambda bi, ci: (0, 0))],
        out_specs=[pl.BlockSpec((None, rows, BR_W), lambda bi, ci: (bi, ci, 0)),
                   pl.BlockSpec((None, H_C, DK_C, DV_C), lambda bi, ci: (bi, 0, 0, 0))],
        out_shape=[jax.ShapeDtypeStruct((bsz, t, BR_W), F32), jax.ShapeDtypeStruct((bsz, H_C, DK_C, DV_C), F32)],
        scratch_shapes=[pltpu.VMEM((H_C, DK_C, DV_C), F32)] + [pltpu.VMEM((SUBLANES, BR_W), F32)] * 3,
        compiler_params=_cparams("parallel", "arbitrary"),
    )(proj, proj, proj, proj, proj, *prm, norm_g.reshape(1, DV_C))


def _gdn_step_kernel(q_ref, k_ref, v_ref, eg_ref, beta_ref, z_ref, g_ref, s_ref, prev_ref, o_ref, so_ref):
    del prev_ref
    nb = s_ref.shape[0]
    eye = (_iota((DK_C, DK_C), 0) == _iota((DK_C, DK_C), 1)).astype(F32)

    def body(bi, carry):
        for h in range(H_C):
            s = s_ref[bi, h]
            eg = eg_ref[bi, h]
            k_col = jnp.sum(eye * k_ref[bi, h], axis=1, keepdims=True)
            q_col = jnp.sum(eye * q_ref[bi, h], axis=1, keepdims=True)
            ks = jnp.sum(s * k_col, axis=0, keepdims=True)
            u = beta_ref[bi, h] * (v_ref[bi, h] - eg * ks)
            s2 = s * eg + k_col * u
            o = jnp.sum(s2 * q_col, axis=0, keepdims=True)
            o_ref[bi, h] = _gdn_out_norm(o, g_ref[...], z_ref[bi, h])
            so_ref[bi, h] = s2
        return carry
    lax.fori_loop(0, nb, body, 0)


def _gdn_step(q, k, v, eg, beta, z, norm_g, state, layer, nb, prev):
    n = q.shape[0]
    rows = [x.reshape(n, H_C, 1, DV_C) for x in (q, k, v, eg, beta, z)]
    row = pl.BlockSpec((nb, H_C, 1, DV_C), lambda i: (i, 0, 0, 0))
    st = _layer_slot_spec(state, layer, nb)
    o, s_new = pl.pallas_call(
        _gdn_step_kernel,
        grid=(n // nb,),
        in_specs=[row] * 6 + [pl.BlockSpec((1, DV_C), lambda i: (0, 0)), st, pl.BlockSpec(memory_space=pl.ANY)],
        out_specs=[row, st],
        out_shape=[jax.ShapeDtypeStruct((n, H_C, 1, DV_C), F32), jax.ShapeDtypeStruct(prev.shape, prev.dtype)],
        input_output_aliases={8: 1},
        compiler_params=_cparams("parallel"),
    )(*rows, norm_g.reshape(1, DV_C), state, prev)
    return o.reshape(n, BR_W), s_new


def _diff_lambda(lam_ref, lam_init):
    lv = lam_ref[...]
    s1 = jnp.sum(lv[0:1] * lv[1:2], axis=1, keepdims=True)
    s2 = jnp.sum(lv[2:3] * lv[3:4], axis=1, keepdims=True)
    return jnp.exp(s1) - jnp.exp(s2) + lam_init


def _diff_out_norm(o, g, lam_init):
    return o * lax.rsqrt(jnp.mean(o * o, axis=1, keepdims=True) + 1e-5) * g * (1.0 - lam_init)


def _row_reduce(x, op):
    rows, cols = x.shape
    part = op(x.reshape(REDUCE_WAYS, rows // (REDUCE_WAYS * SUBLANES), SUBLANES, cols), axis=1)
    return op(op(part, axis=0), axis=0, keepdims=True)


def _diff_flash_kernel(q_ref, k_ref, v_ref, lam_ref, g_ref, o_ref, qs_ref, m_ref, l_ref, acc_ref, s_ref, *, lam_init):
    i = pl.program_id(2)
    tq = q_ref.shape[0]
    q = q_ref[...] * (DK_B ** -0.5 * LOG2_E)
    lane = _iota(q.shape, 1)
    qs_ref[:tq] = jnp.where(lane < DK_B, q, 0.0).astype(BF16)
    qs_ref[tq:] = jnp.where(lane >= DK_B, q, 0.0).astype(BF16)
    m_ref[...] = jnp.full_like(m_ref, NEG_INF)
    l_ref[...] = jnp.zeros_like(l_ref)
    acc_ref[...] = jnp.zeros_like(acc_ref)

    n_grp = 2 * tq // FLASH_COLS
    grp = [slice(g * FLASH_COLS, (g + 1) * FLASH_COLS) for g in range(n_grp)]

    def kv_rows(j):
        return pl.ds(pl.multiple_of(j * tq, tq), tq)

    def scores(j):
        s_ref[j % 2] = _dot_nt(k_ref[kv_rows(j), :].astype(BF16), qs_ref[...])

    def consume(j, diagonal):
        v_t = v_ref[kv_rows(j), :].T.astype(BF16)
        slot = j % 2
        s = [s_ref[slot, :, sl] for sl in grp]
        if diagonal:
            row = _iota(s[0].shape, 0)
            col = _iota(s[0].shape, 1)
            s = [jnp.where(row <= ((col + g * FLASH_COLS) & (tq - 1)), z, NEG_INF) for g, z in enumerate(s)]
        m_old = [m_ref[:, sl] for sl in grp]
        m_new = [jnp.maximum(mo, _row_reduce(z, jnp.max)) for mo, z in zip(m_old, s)]
        p = [jnp.exp2(z - mn) for z, mn in zip(s, m_new)]
        pv = [_dot(v_t, z.astype(BF16)) for z in p]
        for sl, mo, mn, z, zv in zip(grp, m_old, m_new, p, pv):
            alpha = jnp.exp2(mo - mn)
            l_ref[:, sl] = alpha * l_ref[:, sl] + _row_reduce(z, jnp.sum)
            acc_ref[:, sl] = alpha * acc_ref[:, sl] + zv
            m_ref[:, sl] = mn

    scores(0)

    def body(j, carry):
        consume(j - 1, False)
        scores(j)
        return carry
    lax.fori_loop(1, i + 1, body, 0)
    consume(i, True)
    lam = _diff_lambda(lam_ref, lam_init)
    on = acc_ref[...] / l_ref[...]
    o = (on[:, :tq] - lam * on[:, tq:]).T
    o_ref[...] = _diff_out_norm(o, g_ref[...], lam_init)


def _diff_flash(proj, lam, norm_g, lam_init, tq):
    bsz, t, _ = proj.shape
    kq, kk, kv = (BLK512[n] * (BR_W // LANES) for n in ('b_q', 'b_k', 'b_v'))
    return pl.pallas_call(
        functools.partial(_diff_flash_kernel, lam_init=lam_init),
        grid=(bsz, H_B, t // tq),
        in_specs=[pl.BlockSpec((None, tq, LANES), lambda b, h, i: (b, i, kq + h)),
                  pl.BlockSpec((None, t, LANES), lambda b, h, i: (b, 0, kk + h)),
                  pl.BlockSpec((None, t, LANES), lambda b, h, i: (b, 0, kv + h)),
                  pl.BlockSpec((4, DK_B), lambda b, h, i: (0, 0)),
                  pl.BlockSpec((1, DV_B), lambda b, h, i: (0, 0))],
        out_specs=pl.BlockSpec((None, tq, LANES), lambda b, h, i: (b, i, h)),
        out_shape=jax.ShapeDtypeStruct((bsz, t, BR_W), F32),
        scratch_shapes=[pltpu.VMEM((2 * tq, LANES), BF16), pltpu.VMEM((1, 2 * tq), F32),
                        pltpu.VMEM((1, 2 * tq), F32), pltpu.VMEM((DV_B, 2 * tq), F32),
                        pltpu.VMEM((2, tq, 2 * tq), F32)],
        compiler_params=_cparams("parallel", "parallel", "arbitrary"),
    )(proj, proj, proj, lam, norm_g.reshape(1, DV_B))


def _head_rows(x, n_rep):
    rows = []
    for h in range(x.shape[1] // LANES):
        rows += [x[:, h * LANES:(h + 1) * LANES]] * n_rep
    return jnp.concatenate(rows, axis=0)


def _diff_decode_kernel(pt_ref, q_ref, kn_ref, vn_ref, *rest, lam_init, npg):
    del pt_ref
    k_refs, v_refs = rest[:npg], rest[npg:2 * npg]
    lam_ref, g_ref, o_ref = rest[2 * npg:]
    nrow = 2 * H_B
    q8 = _head_rows(q_ref[...] * (DK_B ** -0.5), 2)
    lane = _iota(q8.shape, 1)
    row = _iota(q8.shape, 0)
    q8 = jnp.where((lane >= DK_B) == ((row & 1) == 1), q8, 0.0)
    q8b = q8.astype(BF16)
    ncol = k_refs[0].shape[0]
    valid = (_iota((nrow, ncol), 1) & (H_B - 1)) == (_iota((nrow, ncol), 0) >> 1)
    s_new = jnp.sum(q8 * _head_rows(kn_ref[...], 2), axis=1, keepdims=True)
    s_pages = [jnp.where(valid, _dot_nt(q8b, k[...].astype(BF16)), NEG_INF) for k in k_refs]
    m = s_new
    for s in s_pages:
        m = jnp.maximum(m, jnp.max(s, axis=1, keepdims=True))
    p_new = jnp.exp(s_new - m)
    l_sum = p_new
    acc = p_new * _head_rows(vn_ref[...], 2)
    for s, v in zip(s_pages, v_refs):
        p = jnp.exp(s - m)
        l_sum = l_sum + jnp.sum(p, axis=1, keepdims=True)
        acc = acc + _dot(p.astype(BF16), v[...].astype(BF16))
    on = acc / l_sum
    lam = _diff_lambda(lam_ref, lam_init)
    for h in range(H_B):
        o = on[2 * h:2 * h + 1] - lam * on[2 * h + 1:2 * h + 2]
        o_ref[:, h * DV_B:(h + 1) * DV_B] = _diff_out_norm(o, g_ref[...], lam_init)


def _diff_decode(proj, cache_k, cache_v, page_table, lam, norm_g, lam_init, layer):
    n = proj.shape[0]
    npg = page_table.shape[1]

    def col(name):
        return pl.BlockSpec((None, 1, BR_W), lambda b, pt: (b, 0, BLK512[name]))

    def page(i):
        return pl.BlockSpec((None, None, PAGE * H_B, DV_B), lambda b, pt: (layer, pt[b * npg + i], 0, 0))
    pages = [page(i) for i in range(npg)]
    grid_spec = pltpu.PrefetchScalarGridSpec(
        num_scalar_prefetch=1,
        grid=(n,),
        in_specs=[col('b_q'), col('b_k'), col('b_v')] + pages + pages
                 + [pl.BlockSpec((4, DK_B), lambda b, pt: (0, 0)), pl.BlockSpec((1, DV_B), lambda b, pt: (0, 0))],
        out_specs=pl.BlockSpec((None, 1, BR_W), lambda b, pt: (b, 0, 0)))
    return pl.pallas_call(
        functools.partial(_diff_decode_kernel, lam_init=lam_init, npg=npg),
        grid_spec=grid_spec,
        out_shape=jax.ShapeDtypeStruct((n, 1, BR_W), F32),
        compiler_params=_cparams("parallel"),
    )(page_table.reshape(-1), proj, proj, proj, *([cache_k] * npg), *([cache_v] * npg), lam,
      norm_g.reshape(1, DV_B))


def _cross_kernel(q_ref, k_ref, v_ref, o_ref):
    hsl = [slice(h * HD_X, (h + 1) * HD_X) for h in range(H_X)]
    s = [_dot_nt(q_ref[:, sl].astype(BF16), k_ref[:, sl].astype(BF16)) * (HD_X ** -0.5) for sl in hsl]
    p = [jnp.exp(z - jnp.max(z, axis=1, keepdims=True)) for z in s]
    o = [_dot(z.astype(BF16), v_ref[:, sl].astype(BF16)) for z, sl in zip(p, hsl)]
    for sl, z, oz in zip(hsl, p, o):
        o_ref[:, sl] = oz / jnp.sum(z, axis=1, keepdims=True)


def _cross_prompt(proj, mem_kv, tq):
    bsz, t, _ = proj.shape
    n_mem = mem_kv.shape[1]
    return pl.pallas_call(
        _cross_kernel,
        grid=(bsz, t // tq),
        in_specs=[pl.BlockSpec((None, tq, BR_W), lambda b, i: (b, i, BLK512['x_q'])),
                  pl.BlockSpec((None, n_mem, BR_W), lambda b, i: (b, 0, 0)),
                  pl.BlockSpec((None, n_mem, BR_W), lambda b, i: (b, 0, 1))],
        out_specs=pl.BlockSpec((None, tq, BR_W), lambda b, i: (b, i, 0)),
        out_shape=jax.ShapeDtypeStruct((bsz, t, BR_W), F32),
        compiler_params=_cparams("parallel", "parallel"),
    )(proj, mem_kv, mem_kv)


def _cross_decode_kernel(q_ref, k_ref, v_ref, o_ref):
    nb = q_ref.shape[0]
    ncol = k_ref.shape[1]
    valid = (_iota((SUBLANES, ncol), 1) & (H_X - 1)) == _iota((SUBLANES, ncol), 0)
    for b in range(nb):
        q8 = jnp.concatenate([_head_rows(q_ref[b], 1), jnp.zeros((SUBLANES - H_X, HD_X), F32)], axis=0)
        s = _dot_nt(q8.astype(BF16), k_ref[b].astype(BF16)) * (HD_X ** -0.5)
        s = jnp.where(valid, s, NEG_INF)
        m = jnp.max(s, axis=1, keepdims=True)
        p = jnp.exp(s - m)
        o = _dot(p.astype(BF16), v_ref[b].astype(BF16)) / jnp.sum(p, axis=1, keepdims=True)
        for h in range(H_X):
            o_ref[b, :, h * HD_X:(h + 1) * HD_X] = o[h:h + 1]


def _cross_decode(proj, cache_k, cache_v, layer, nb):
    n = proj.shape[0]
    rows = cache_k.shape[2]
    kv = pl.BlockSpec((None, nb, rows, HD_X), lambda i: (layer, i, 0, 0))
    return pl.pallas_call(
        _cross_decode_kernel,
        grid=(n // nb,),
        in_specs=[pl.BlockSpec((nb, 1, BR_W), lambda i: (i, 0, BLK512['x_q'])), kv, kv],
        out_specs=pl.BlockSpec((nb, 1, BR_W), lambda i: (i, 0, 0)),
        out_shape=jax.ShapeDtypeStruct((n, 1, BR_W), F32),
        compiler_params=_cparams("parallel"),
    )(proj, cache_k, cache_v)


def _merge_kernel(x_ref, oa_ref, az_ref, ob_ref, bz_ref, oc_ref, ox_ref, xz_ref, mg_ref, wb_ref, wo_ref, o_ref):
    branches = (oa_ref[...] * _silu(az_ref[...]), ob_ref[...] * _silu(bz_ref[...]), oc_ref[...],
                ox_ref[...] * _silu(xz_ref[...]))
    acc = None
    for n, br in enumerate(branches):
        gate = _sigmoid(mg_ref[:, n * D_MODEL:(n + 1) * D_MODEL])
        term = gate * _dot(br.astype(BF16), wb_ref[n])
        acc = term if acc is None else acc + term
    o_ref[...] = x_ref[...] + _dot(acc.astype(BF16), wo_ref[...])


def _merge(x, proj, o_a, o_b, o_c, o_x, w_branch, w_out, tm):
    n = x.shape[0]

    def col(name):
        return pl.BlockSpec((tm, BR_W), lambda i: (i, BLK512[name]))
    br = pl.BlockSpec((tm, BR_W), lambda i: (i, 0))
    return pl.pallas_call(
        _merge_kernel,
        grid=(n // tm,),
        in_specs=[pl.BlockSpec((tm, D_MODEL), lambda i: (i, 0)),
                  br, col('a_z'), br, col('b_z'), br, br, col('x_z'),
                  pl.BlockSpec((tm, MERGE_W), lambda i: (i, 0)),
                  pl.BlockSpec((N_BRANCH, BR_W, D_MODEL), lambda i: (0, 0, 0)),
                  pl.BlockSpec((D_MODEL, D_MODEL), lambda i: (0, 0))],
        out_specs=pl.BlockSpec((tm, D_MODEL), lambda i: (i, 0)),
        out_shape=jax.ShapeDtypeStruct((n, D_MODEL), F32),
        compiler_params=_cparams("parallel"),
    )(x, o_a, proj, o_b, proj, o_c, o_x, proj, proj, w_branch, w_out)


def _kv_layout_kernel(k_ref, v_ref, pk_ref, pv_ref, ok_ref, ov_ref):
    del pk_ref, pv_ref
    for src, dst in ((k_ref, ok_ref), (v_ref, ov_ref)):
        for h in range(H_B):
            dst[:, h, :] = src[:, h * LANES:(h + 1) * LANES]


def _kv_layout(proj, slot, prev, tm):
    bsz, t, _ = proj.shape
    out = pl.BlockSpec((None, None, tm, H_B, LANES), lambda b, i: (slot, b, i, 0, 0))
    return pl.pallas_call(
        _kv_layout_kernel,
        grid=(bsz, t // tm),
        in_specs=[pl.BlockSpec((None, tm, BR_W), lambda b, i: (b, i, BLK512['b_k'])),
                  pl.BlockSpec((None, tm, BR_W), lambda b, i: (b, i, BLK512['b_v'])),
                  pl.BlockSpec(memory_space=pl.ANY), pl.BlockSpec(memory_space=pl.ANY)],
        out_specs=[out, out],
        out_shape=[jax.ShapeDtypeStruct(z.shape, z.dtype) for z in prev],
        input_output_aliases={2: 0, 3: 1},
        compiler_params=_cparams("parallel", "parallel"),
    )(proj, proj, *prev)


def _merge_token_head(cache):
    return cache.reshape(*cache.shape[:-3], cache.shape[-3] * cache.shape[-2], cache.shape[-1])


def _permute_w_in(w_in):
    d = w_in.shape[0]
    parts = [w_in[..., REF_OFF['merge']:REF_OFF['merge'] + MERGE_W]]
    parts += [w_in[..., REF_OFF[n]:REF_OFF[n] + BR_W] for n in W512]
    parts.append(w_in[..., REF_OFF['a_lora']:REF_OFF['a_lora'] + 2 * LORA])
    parts.append(w_in[..., REF_OFF['c_ab']:REF_OFF['c_ab'] + 2 * H_C])
    parts.append(jnp.zeros(w_in.shape[:-1] + (LANES - 2 * H_C,), w_in.dtype))
    del d
    return jnp.concatenate(parts, axis=-1).astype(BF16)


def _layer_params(p, l):
    mu = p['rwkv_mu'][l]
    row = lambda z: z.reshape(1, -1)
    zeros64 = jnp.zeros((LORA, BR_W), F32)
    rwkv = (row(mu[:BR_W]), row(mu[BR_W:2 * BR_W]), row(mu[2 * BR_W:3 * BR_W]), row(mu[3 * BR_W:]),
            row(p['rwkv_w0'][l]), jnp.concatenate([p['rwkv_w2'][l], zeros64], axis=0),
            row(p['rwkv_a0'][l]), jnp.concatenate([zeros64, p['rwkv_a2'][l]], axis=0),
            row(p['rwkv_kk'][l]), row(p['rwkv_ka'][l]))
    wc = p['gdn_conv_w'][l]
    pad = lambda z: jnp.pad(z, (0, LANES - z.shape[0])).reshape(1, LANES)
    gdn = (wc[:, :BR_W], wc[:, BR_W:2 * BR_W], wc[:, 2 * BR_W:], pad(p['gdn_a_log'][l]), pad(p['gdn_dt_bias'][l]))
    return rwkv, gdn


def _prompt_layer(x, mem, p, wts, l, kv, slot):
    bsz, t, d = x.shape
    n = bsz * t
    rwkv_prm, gdn_prm = _layer_params(p, l)
    lam_init = 0.8 - 0.6 * math.exp(-0.3 * l)
    mem_kv = _norm_matmul(mem.reshape(-1, d), p['norm_mem_g'][l], wts['w_mem'][l], 256, 512)
    mem_kv = mem_kv.reshape(bsz, -1, 2 * BR_W)
    proj = _norm_matmul(x.reshape(n, d), p['norm_g'][l], wts['w_in'][l], min(PROJ_TM, n), PROJ_TN)
    proj3 = proj.reshape(bsz, t, PROJ_W)
    row = lambda z: z.reshape(1, -1)
    o_a, s_rwkv = _rwkv_chunk(proj3, rwkv_prm, row(p['rwkv_ln_g'][l]), row(p['rwkv_ln_b'][l]), row(p['rwkv_rk'][l]))
    o_b = _diff_flash(proj3, p['diff_lam'][l], p['diff_norm_g'][l], lam_init, min(FLASH_TQ, t // 2))
    o_c, s_gdn = _gdn_chunk(proj3, gdn_prm, p['gdn_norm_g'][l])
    o_x = _cross_prompt(proj3, mem_kv, min(512, t))
    flat = lambda z: z.reshape(n, BR_W)
    x_new = _merge(x.reshape(n, d), proj, flat(o_a), flat(o_b), flat(o_c), flat(o_x), wts['w_branch'][l],
                   wts['w_out'][l], min(MERGE_TM, n))

    def cols(name, width=BR_W):
        off = BLK512[name] * BR_W
        return proj3[:, :, off:off + width]
    kv = _kv_layout(proj3, slot, kv, min(KV_TM, t))
    last = proj3[:, t - 1:, :]
    a_off = BLK512['a_r'] * BR_W
    shift = jnp.concatenate([last[:, :, a_off:a_off + 3 * BR_W],
                             last[:, :, LORA_BLK * LANES:LORA_BLK * LANES + 2 * LORA]], axis=-1)
    c_off = BLK512['c_q'] * BR_W
    conv = proj3[:, t - (CONV_W - 1):, c_off:c_off + 3 * BR_W]
    mk = mem_kv[:, :, :BR_W].reshape(bsz, -1, H_X, HD_X)
    mv = mem_kv[:, :, BR_W:].reshape(bsz, -1, H_X, HD_X)
    return x_new.reshape(bsz, t, d), (s_rwkv, shift, s_gdn, conv, mk, mv), kv


def _sample_layer(x, caches, p, wts, l, prev):
    n, _, d = x.shape
    (cache_k, cache_v, page_table, cache_mk, cache_mv, st_rwkv, st_shift, st_gdn, st_conv) = caches
    rwkv_prm, gdn_prm = _layer_params(p, l)
    lam_init = 0.8 - 0.6 * math.exp(-0.3 * l)
    proj = _norm_matmul(x.reshape(n, d), p['norm_g'][l], wts['w_in'][l], n, PROJ_TN)
    proj3 = proj.reshape(n, 1, PROJ_W)
    sh = st_shift[l].reshape(n, -1)
    shift_in = (sh[:, :BR_W], sh[:, BR_W:2 * BR_W], sh[:, 2 * BR_W:3 * BR_W], sh[:, 3 * BR_W:])
    rows = _rwkv_prep_sample(proj, shift_in, rwkv_prm)
    o_a, s_rwkv = _rwkv_step(rows, p['rwkv_ln_g'][l], p['rwkv_ln_b'][l], p['rwkv_rk'][l], st_rwkv, l, 8, prev[0])
    o_b = _diff_decode(proj3, cache_k, cache_v, page_table, p['diff_lam'][l], p['diff_norm_g'][l], lam_init, l)
    buf = jnp.moveaxis(st_conv[l], 1, 0)
    bufs = (buf[:, :, :BR_W], buf[:, :, BR_W:2 * BR_W], buf[:, :, 2 * BR_W:])
    gq, gk, gv, eg, beta = _gdn_prep_sample(proj, bufs, gdn_prm)
    z_off = BLK512['c_z'] * BR_W
    o_c, s_gdn = _gdn_step(gq, gk, gv, eg, beta, proj[:, z_off:z_off + BR_W], p['gdn_norm_g'][l], st_gdn, l, 4,
                           prev[1])
    o_x = _cross_decode(proj3, cache_mk, cache_mv, l, 4)
    x_new = _merge(x.reshape(n, d), proj, o_a, o_b.reshape(n, BR_W), o_c, o_x.reshape(n, BR_W), wts['w_branch'][l],
                   wts['w_out'][l], n)

    def cols(name, width=BR_W):
        off = BLK512[name] * BR_W
        return proj[:, off:off + width]
    k_new = cols('b_k').reshape(n, 1, H_B, 2 * DK_B)
    v_new = cols('b_v').reshape(n, 1, H_B, DV_B)
    shift = jnp.concatenate([cols('a_r', 3 * BR_W), proj[:, LORA_BLK * LANES:LORA_BLK * LANES + 2 * LORA]],
                            axis=-1).reshape(n, 1, -1)
    conv = jnp.concatenate([st_conv[l][:, 1:], cols('c_q', 3 * BR_W).reshape(n, 1, -1)], axis=1)
    return x_new.reshape(n, 1, d), (k_new, v_new, s_rwkv, shift, s_gdn, conv)


def kernel(x_prompt, x_sample, mem_prompt, cache_diff_k, cache_diff_v, page_table, cache_mem_k, cache_mem_v,
           state_rwkv, state_rwkv_shift, state_gdn, state_gdn_conv, norm_g, w_in, rwkv_mu, rwkv_w0, rwkv_w2,
           rwkv_a0, rwkv_a2, rwkv_kk, rwkv_ka, rwkv_rk, rwkv_ln_g, rwkv_ln_b, diff_lam, diff_norm_g, gdn_conv_w,
           gdn_a_log, gdn_dt_bias, gdn_norm_g, norm_mem_g, w_mem_kv, w_branch, w_out, final_norm_g):
    p = {'norm_g': norm_g, 'rwkv_mu': rwkv_mu, 'rwkv_w0': rwkv_w0, 'rwkv_w2': rwkv_w2, 'rwkv_a0': rwkv_a0,
         'rwkv_a2': rwkv_a2, 'rwkv_kk': rwkv_kk, 'rwkv_ka': rwkv_ka, 'rwkv_rk': rwkv_rk, 'rwkv_ln_g': rwkv_ln_g,
         'rwkv_ln_b': rwkv_ln_b, 'diff_lam': diff_lam, 'diff_norm_g': diff_norm_g, 'gdn_conv_w': gdn_conv_w,
         'gdn_a_log': gdn_a_log, 'gdn_dt_bias': gdn_dt_bias, 'gdn_norm_g': gdn_norm_g, 'norm_mem_g': norm_mem_g}
    depth = w_in.shape[0]
    wts = {'w_in': _permute_w_in(w_in), 'w_mem': w_mem_kv.astype(BF16), 'w_branch': w_branch.astype(BF16),
           'w_out': w_out.astype(BF16)}
    n_s = x_sample.shape[0]
    caches = (_merge_token_head(cache_diff_k), _merge_token_head(cache_diff_v), page_table,
              _merge_token_head(cache_mem_k), _merge_token_head(cache_mem_v), state_rwkv, state_rwkv_shift, state_gdn,
              state_gdn_conv)
    xp, xs = x_prompt, x_sample
    outs_p, outs_s = [], []
    bsz_p, t_p = x_prompt.shape[:2]
    kv = tuple(jnp.zeros((depth, bsz_p, t_p, H_B, LANES), F32) for _ in range(2))
    stacked = (jnp.zeros_like(state_rwkv), jnp.zeros_like(state_gdn))
    for l in range(depth):
        xp, st_p, kv = _prompt_layer(xp, mem_prompt, p, wts, l, kv, l)
        xs, st_s = _sample_layer(xs, caches, p, wts, l, stacked)
        stacked = (st_s[2], st_s[4])
        outs_p.append(st_p)
        outs_s.append(st_s)
    bsz, t, d = xp.shape
    y_prompt = _final_norm(xp.reshape(bsz * t, d), final_norm_g, min(512, bsz * t)).reshape(bsz, t, d)
    y_sample = _final_norm(xs.reshape(n_s, d), final_norm_g, n_s).reshape(n_s, 1, d)
    diff_k_prompt, diff_v_prompt = kv
    sp = lambda i: jnp.stack([o[i] for o in outs_p])
    ss = lambda i: jnp.stack([o[i] for o in outs_s])
    return (y_prompt, y_sample, diff_k_prompt, diff_v_prompt, ss(0), ss(1), sp(4), sp(5), sp(0), stacked[0], sp(1),
            ss(3), sp(2), stacked[1], sp(3), ss(5))
```
